```python
import jax
import jax.numpy as jnp
from jax import lax
import numpy as np

D_MODEL = 1024
BATCH = 16
SEQ = 256
DEPTH = 4
DEC_BATCH = 4
DEC_SEQ = 2048
PAST_LEN = 256

GRID_W = 64
HEAD_DIM = 64
CONV_CH = D_MODEL // 4
ATT_WIDTH = D_MODEL // 2
ATT_HEADS = ATT_WIDTH // HEAD_DIM
ATT_KV_HEADS = ATT_HEADS // 4
ATT_GQ = ATT_HEADS // ATT_KV_HEADS
KV_WIDTH = ATT_KV_HEADS * HEAD_DIM
WINDOW = 128
ATT_BLOCK = 128
RW_WIDTH = D_MODEL // 4
RW_HEADS = RW_WIDTH // HEAD_DIM
W_LORA = 64
A_LORA = 64
MIX_WIDTH = CONV_CH + ATT_WIDTH + RW_WIDTH
SPLIT_POINTS = [CONV_CH, 2 * CONV_CH, 3 * CONV_CH, 3 * CONV_CH + ATT_WIDTH,
                3 * CONV_CH + ATT_WIDTH + KV_WIDTH, 3 * CONV_CH + ATT_WIDTH + 2 * KV_WIDTH]
IN_WIDTH = SPLIT_POINTS[-1] + 5 * RW_WIDTH
N_GROUPS = 4
EXP_PER_GROUP = 8
N_EXPERTS = N_GROUPS * EXP_PER_GROUP
TOP_K = 2
D_EXPERT = D_MODEL // 2
MOE_BLOCK = 128
ROPE_BASE = 10000.0
ROT_HALF = HEAD_DIM // 2
NORM_EPS = 1e-6
GN_EPS = 64e-5
DECAY_SCALE = 0.6065306597126334
NEG_INF = -1e30

kernel_name = 'hybrid_dit_conv_swa_rwkv7_hmoe_step'


def rms_norm(x, g):
    xf = x.astype(jnp.float32)
    y = xf * lax.rsqrt(jnp.mean(xf * xf, axis=-1, keepdims=True) + NORM_EPS)
    return (y * g.astype(jnp.float32)).astype(x.dtype)


def axial_rope(x):
    n_tok = x.shape[1]
    n_rows = n_tok // GRID_W
    rows = jnp.repeat(jnp.arange(n_rows, dtype=jnp.float32), GRID_W)
    cols = (jnp.arange(n_tok) % GRID_W).astype(jnp.float32)
    inv = ROPE_BASE ** (-jnp.arange(0, ROT_HALF, 2, dtype=jnp.float32) / ROT_HALF)

    def rot(xh, pos):
        ang = pos[:, None] * inv[None, :]
        cos = jnp.cos(ang)[None, :, None, :]
        sin = jnp.sin(ang)[None, :, None, :]
        x1, x2 = xh[..., :ROT_HALF // 2], xh[..., ROT_HALF // 2:]
        return jnp.concatenate([x1 * cos - x2 * sin, x2 * cos + x1 * sin], axis=-1)

    xf = x.astype(jnp.float32)
    out = jnp.concatenate([rot(xf[..., :ROT_HALF], rows), rot(xf[..., ROT_HALF:], cols)], axis=-1)
    return out.astype(x.dtype)


def sink_softmax(scores, sink):
    s_sink = jnp.broadcast_to(sink.astype(jnp.float32).reshape(ATT_KV_HEADS, ATT_GQ, 1, 1),
                              scores.shape[:-1] + (1,))
    p = jax.nn.softmax(jnp.concatenate([scores, s_sink], axis=-1), axis=-1)
    return p[..., :-1]


def context_attention(q, k, v, sink):
    bsz, n_ctx = q.shape[:2]
    nq = n_ctx // ATT_BLOCK
    scale = HEAD_DIM ** -0.5
    qb = jnp.moveaxis(q.reshape(bsz, nq, ATT_BLOCK, ATT_KV_HEADS, ATT_GQ, HEAD_DIM), 1, 0)

    def one_block(q_blk):
        s = jnp.einsum('bqhgd,bchd->bhgqc', q_blk, k).astype(jnp.float32) * scale
        p = sink_softmax(s, sink).astype(v.dtype)
        return jnp.einsum('bhgqc,bchd->bqhgd', p, v)

    o = lax.map(one_block, qb)
    return jnp.moveaxis(o, 0, 1).reshape(bsz, n_ctx, ATT_WIDTH)


def latent_window_attention(q, k, v, k_ctx, v_ctx, sink):
    bsz, n_tok = q.shape[:2]
    nb = n_tok // ATT_BLOCK
    scale = HEAD_DIM ** -0.5
    qb = q.reshape(bsz, nb, ATT_BLOCK, ATT_KV_HEADS, ATT_GQ, HEAD_DIM)
    pad = ((0, 0), (ATT_BLOCK, ATT_BLOCK), (0, 0), (0, 0))
    kp = jnp.pad(k, pad).reshape(bsz, nb + 2, ATT_BLOCK, ATT_KV_HEADS, HEAD_DIM)
    vp = jnp.pad(v, pad).reshape(bsz, nb + 2, ATT_BLOCK, ATT_KV_HEADS, HEAD_DIM)
    k_win = jnp.concatenate([kp[:, :-2], kp[:, 1:-1], kp[:, 2:]], axis=2)
    v_win = jnp.concatenate([vp[:, :-2], vp[:, 1:-1], vp[:, 2:]], axis=2)
    s_loc = jnp.einsum('bnqhgd,bnkhd->bnhgqk', qb, k_win).astype(jnp.float32) * scale
    q_idx = jnp.arange(ATT_BLOCK)
    k_idx = jnp.arange(3 * ATT_BLOCK)
    rel = k_idx[None, :] - ATT_BLOCK - q_idx[:, None]
    k_pos = jnp.arange(nb)[:, None] * ATT_BLOCK - ATT_BLOCK + k_idx[None, :]
    valid = (jnp.abs(rel) <= WINDOW)[None] & ((k_pos >= 0) & (k_pos < n_tok))[:, None, :]
    s_loc = jnp.where(valid[None, :, None, None], s_loc, NEG_INF)
    s_ctx = jnp.einsum('bnqhgd,bchd->bnhgqc', qb, k_ctx).astype(jnp.float32) * scale
    p = sink_softmax(jnp.concatenate([s_loc, s_ctx], axis=-1), sink)
    n_loc = 3 * ATT_BLOCK
    p_loc = p[..., :n_loc].astype(v.dtype)
    p_ctx = p[..., n_loc:].astype(v.dtype)
    o = (jnp.einsum('bnhgqk,bnkhd->bnqhgd', p_loc, v_win)
         + jnp.einsum('bnhgqc,bchd->bnqhgd', p_ctx, v_ctx))
    return o.reshape(bsz, n_tok, ATT_WIDTH)


def short_conv_mixer(b_gate, c_gate, h, conv_w):
    u = c_gate * h
    up = jnp.pad(u, ((0, 0), (1, 1), (0, 0)))
    conv = up[:, :-2] * conv_w[0] + up[:, 1:-1] * conv_w[1] + up[:, 2:] * conv_w[2]
    return b_gate * conv


def wkv_scan(r, w, k, v, a, b, s0, reverse):
    def step(s, inp):
        r_t, w_t, k_t, v_t, a_t, b_t = inp
        sa = jnp.einsum('bhvk,bhk->bhv', s, a_t)
        s = (s * w_t[:, :, None, :] + sa[..., None] * b_t[:, :, None, :]
             + v_t[..., None] * k_t[:, :, None, :])
        return s, jnp.einsum('bhvk,bhk->bhv', s, r_t)

    xs = tuple(jnp.moveaxis(t, 1, 0) for t in (r, w, k, v, a, b))
    s_fin, ys = lax.scan(step, s0, xs, reverse=reverse)
    return jnp.moveaxis(ys, 0, 1), s_fin


def rwkv7_mixer(slab, mu, w0, w1, w2, a0, a1, a2, k_k, k_a, r_k, ln_g, ln_b, init_state):
    f32 = jnp.float32
    bsz, n_tok, _ = slab.shape
    hs = (bsz, n_tok, RW_HEADS, HEAD_DIM)
    xin = slab[..., :4 * RW_WIDTH]
    g = jax.nn.sigmoid(slab[..., 4 * RW_WIDTH:].astype(f32))
    y_sum = jnp.zeros(hs, f32)
    bonus = jnp.zeros(hs, f32)
    finals = []
    for d in range(2):
        if d == 0:
            nbr = jnp.pad(xin, ((0, 0), (1, 0), (0, 0)))[:, :-1]
        else:
            nbr = jnp.pad(xin, ((0, 0), (0, 1), (0, 0)))[:, 1:]
        xd = xin + (nbr - xin) * mu[d]
        r, k, v, z = jnp.split(xd, 4, axis=-1)
        w = jnp.exp(-DECAY_SCALE * jax.nn.sigmoid((w0[d] + jnp.tanh(z @ w1[d]) @ w2[d]).astype(f32)))
        a = jax.nn.sigmoid((a0[d] + (z @ a1[d]) @ a2[d]).astype(f32))
        kf = k.astype(f32)
        kk = (kf * k_k[d].astype(f32)).reshape(hs)
        kk = kk / jnp.maximum(jnp.sqrt(jnp.sum(kk * kk, axis=-1, keepdims=True)), 1e-12)
        k2 = (kf * (1.0 + (a - 1.0) * k_a[d].astype(f32))).reshape(hs)
        r_h = r.astype(f32).reshape(hs)
        v_h = v.astype(f32).reshape(hs)
        a_h = a.reshape(hs)
        if init_state is None:
            s0 = jnp.zeros((bsz, RW_HEADS, HEAD_DIM, HEAD_DIM), f32)
        else:
            s0 = init_state[:, d].astype(f32)
        y_d, s_d = wkv_scan(r_h, w.reshape(hs), k2, v_h, -kk, kk * a_h, s0, d == 1)
        y_sum = y_sum + y_d
        rk = r_k[d].astype(f32).reshape(RW_HEADS, HEAD_DIM)
        bonus = bonus + jnp.sum(r_h * k2 * rk, axis=-1, keepdims=True) * v_h
        finals.append(s_d)
    mean = jnp.mean(y_sum, axis=-1, keepdims=True)
    var = jnp.mean(jnp.square(y_sum - mean), axis=-1, keepdims=True)
    yn = ((y_sum - mean) * lax.rsqrt(var + GN_EPS)).reshape(bsz, n_tok, RW_WIDTH)
    yn = yn * ln_g.astype(f32) + ln_b.astype(f32)
    out = (yn + bonus.reshape(bsz, n_tok, RW_WIDTH)) * g
    return out.astype(slab.dtype), jnp.stack(finals, axis=1)


def grouped_expert_mlp(xf, e_idx, e_w, w1, w3, w2):
    n_tok, d = xf.shape
    n_asg = n_tok * TOP_K
    flat_e = e_idx.reshape(-1)
    order = jnp.argsort(flat_e)
    sorted_e = flat_e[order]
    counts = jnp.zeros((N_EXPERTS,), jnp.int32).at[flat_e].add(1)
    padded = (counts + MOE_BLOCK - 1) // MOE_BLOCK * MOE_BLOCK
    pad_end = jnp.cumsum(padded)
    pad_start = pad_end - padded
    start = jnp.cumsum(counts) - counts
    slot = pad_start[sorted_e] + jnp.arange(n_asg, dtype=jnp.int32) - start[sorted_e]
    n_blocks = -(-n_asg // MOE_BLOCK) + N_EXPERTS
    n_slots = n_blocks * MOE_BLOCK
    slot_tok = jnp.full((n_slots,), n_tok, jnp.int32).at[slot].set((order // TOP_K).astype(jnp.int32))
    slot_w = jnp.zeros((n_slots,), xf.dtype).at[slot].set(e_w.reshape(-1)[order])
    block_exp = jnp.minimum(
        jnp.searchsorted(pad_end, jnp.arange(n_blocks, dtype=jnp.int32) * MOE_BLOCK, side='right'),
        N_EXPERTS - 1)
    x_pad = jnp.concatenate([xf, jnp.zeros((1, d), xf.dtype)], axis=0)
    xb = x_pad[slot_tok].reshape(n_blocks, MOE_BLOCK, d)

    def expert_block(args):
        x_blk, e = args
        hid = jax.nn.silu(x_blk @ w1[e]) * (x_blk @ w3[e])
        return hid @ w2[e]

    yb = lax.map(expert_block, (xb, block_exp)).reshape(n_slots, d)
    y = jnp.zeros((n_tok + 1, d), xf.dtype).at[slot_tok].add(yb * slot_w[:, None])
    return y[:n_tok]


def hier_moe(h, rg_w, rg_b, re_w, re_b, w1, w3, w2):
    f32 = jnp.float32
    bsz, n_tok, _ = h.shape
    hf = h.reshape(-1, D_MODEL)
    n = hf.shape[0]
    g_prob = jax.nn.softmax((hf @ rg_w).astype(f32) + rg_b.astype(f32), axis=-1)
    g_top, g_idx = lax.top_k(g_prob, 1)
    e_logits = ((hf @ re_w).astype(f32) + re_b.astype(f32)).reshape(n, N_GROUPS, EXP_PER_GROUP)
    e_logits = jnp.einsum('nge,ng->ne', e_logits, jax.nn.one_hot(g_idx[:, 0], N_GROUPS, dtype=f32))
    e_prob = jax.nn.softmax(e_logits, axis=-1)
    e_top, e_loc = lax.top_k(e_prob, TOP_K)
    weights = g_top * e_top / jnp.sum(e_top, axis=-1, keepdims=True)
    idx = g_idx * EXP_PER_GROUP + e_loc
    y = grouped_expert_mlp(hf, idx, weights.astype(hf.dtype), w1, w3, w2)
    return y.reshape(bsz, n_tok, D_MODEL)


def trunk_layer(x, cond, lw, ctx_k, ctx_v, ctx_state):
    bsz, n_tok, _ = x.shape
    bc = cond.shape[0]
    mod = (jax.nn.silu(cond) @ lw['w_mod'] + lw['b_mod']).reshape(bc, 6, 1, D_MODEL)
    h = rms_norm(x, lw['norm_mix']) * (1.0 + mod[:, 1]) + mod[:, 0]
    proj = h @ lw['w_in']
    cb, cc, ch, q, k, v, rw_slab = jnp.split(proj, SPLIT_POINTS, axis=-1)
    q = q.reshape(bsz, n_tok, ATT_HEADS, HEAD_DIM)
    k = k.reshape(bsz, n_tok, ATT_KV_HEADS, HEAD_DIM)
    v = v.reshape(bsz, n_tok, ATT_KV_HEADS, HEAD_DIM)
    y_conv = short_conv_mixer(cb, cc, ch, lw['conv_w'])
    if ctx_k is None:
        y_att = context_attention(q, k, v, lw['attn_sink'])
    else:
        y_att = latent_window_attention(axial_rope(q), axial_rope(k), v, ctx_k, ctx_v, lw['attn_sink'])
    y_rw, s_fin = rwkv7_mixer(rw_slab, lw['rw_mu'], lw['rw_w0'], lw['rw_w1'], lw['rw_w2'],
                              lw['rw_a0'], lw['rw_a1'], lw['rw_a2'], lw['rw_kk'], lw['rw_ka'],
                              lw['rw_rk'], lw['rw_ln_g'], lw['rw_ln_b'], ctx_state)
    mix = jnp.concatenate([y_conv, y_att, y_rw], axis=-1) @ lw['w_out']
    x = x + mod[:, 2] * mix
    h2 = rms_norm(x, lw['norm_ffn']) * (1.0 + mod[:, 4]) + mod[:, 3]
    x = x + mod[:, 5] * hier_moe(h2, lw['rt_group_w'], lw['rt_group_b'], lw['rt_exp_w'],
                                 lw['rt_exp_b'], lw['exp_w1'], lw['exp_w3'], lw['exp_w2'])
    if ctx_k is None:
        return x, k, v, s_fin
    return x


def setup_inputs(seed: int = 0) -> dict:
    key = jax.random.key(seed)
    ks = list(jax.random.split(key, 40))
    f32 = jnp.float32
    d = D_MODEL

    def nrm(i, shape, scale):
        return jax.random.normal(ks[i], shape, f32) * scale

    return {
        'x_prompt': nrm(0, (BATCH, SEQ, d), 1.0),
        'x_sample': nrm(1, (DEC_BATCH, DEC_SEQ, d), 1.0),
        'cache_k': nrm(2, (DEC_BATCH, DEPTH, PAST_LEN, ATT_KV_HEADS, HEAD_DIM), 1.0),
        'cache_v': nrm(3, (DEC_BATCH, DEPTH, PAST_LEN, ATT_KV_HEADS, HEAD_DIM), 1.0),
        'state_wkv': nrm(4, (DEC_BATCH, DEPTH, 2, RW_HEADS, HEAD_DIM, HEAD_DIM), 0.5),
        'c': nrm(5, (DEC_BATCH, d), 1.0),
        'c_ctx': nrm(6, (d,), 1.0),
        'w_mod': nrm(7, (DEPTH, d, 6 * d), 0.5 * d ** -0.5),
        'b_mod': nrm(8, (DEPTH, 6 * d), 0.02),
        'norm_mix': 1.0 + nrm(9, (DEPTH, d), 0.05),
        'norm_ffn': 1.0 + nrm(10, (DEPTH, d), 0.05),
        'norm_out': 1.0 + nrm(11, (d,), 0.05),
        'w_in': nrm(12, (DEPTH, d, IN_WIDTH), d ** -0.5),
        'w_out': nrm(13, (DEPTH, MIX_WIDTH, d), MIX_WIDTH ** -0.5),
        'conv_w': nrm(14, (DEPTH, 3, CONV_CH), 0.5),
        'attn_sink': nrm(15, (DEPTH, ATT_HEADS), 0.5),
        'rw_mu': jax.random.uniform(ks[16], (DEPTH, 2, 4 * RW_WIDTH), f32),
        'rw_w0': nrm(17, (DEPTH, 2, RW_WIDTH), 1.0),
        'rw_w1': nrm(18, (DEPTH, 2, RW_WIDTH, W_LORA), RW_WIDTH ** -0.5),
        'rw_w2': nrm(19, (DEPTH, 2, W_LORA, RW_WIDTH), 0.5 * W_LORA ** -0.5),
        'rw_a0': nrm(20, (DEPTH, 2, RW_WIDTH), 0.5),
        'rw_a1': nrm(21, (DEPTH, 2, RW_WIDTH, A_LORA), RW_WIDTH ** -0.5),
        'rw_a2': nrm(22, (DEPTH, 2, A_LORA, RW_WIDTH), 0.5 * A_LORA ** -0.5),
        'rw_kk': 0.85 + nrm(23, (DEPTH, 2, RW_WIDTH), 0.05),
        'rw_ka': 1.0 + nrm(24, (DEPTH, 2, RW_WIDTH), 0.05),
        'rw_rk': nrm(25, (DEPTH, 2, RW_WIDTH), 0.1),
        'rw_ln_g': 1.0 + nrm(26, (DEPTH, RW_WIDTH), 0.05),
        'rw_ln_b': nrm(27, (DEPTH, RW_WIDTH), 0.01),
        'rt_group_w': nrm(28, (DEPTH, d, N_GROUPS), d ** -0.5),
        'rt_group_b': nrm(29, (DEPTH, N_GROUPS), 0.01),
        'rt_exp_w': nrm(30, (DEPTH, d, N_EXPERTS), d ** -0.5),
        'rt_exp_b': nrm(31, (DEPTH, N_EXPERTS), 0.01),
        'exp_w1': nrm(32, (DEPTH, N_EXPERTS, d, D_EXPERT), d ** -0.5),
        'exp_w3': nrm(33, (DEPTH, N_EXPERTS, d, D_EXPERT), d ** -0.5),
        'exp_w2': nrm(34, (DEPTH, N_EXPERTS, D_EXPERT, d), D_EXPERT ** -0.5),
    }


def reference(x_prompt, x_sample, cache_k, cache_v, state_wkv, c, c_ctx, w_mod, b_mod,
              norm_mix, norm_ffn, norm_out, w_in, w_out, conv_w, attn_sink, rw_mu, rw_w0,
              rw_w1, rw_w2, rw_a0, rw_a1, rw_a2, rw_kk, rw_ka, rw_rk, rw_ln_g, rw_ln_b,
              rt_group_w, rt_group_b, rt_exp_w, rt_exp_b, exp_w1, exp_w3, exp_w2):
    x_p = x_prompt
    x_s = x_sample
    cond_p = c_ctx[None, :]
    keys_p, vals_p, states_p = [], [], []
    for l in range(DEPTH):
        lw = {
            'w_mod': w_mod[l], 'b_mod': b_mod[l], 'norm_mix': norm_mix[l], 'norm_ffn': norm_ffn[l],
            'w_in': w_in[l], 'w_out': w_out[l], 'conv_w': conv_w[l], 'attn_sink': attn_sink[l],
            'rw_mu': rw_mu[l], 'rw_w0': rw_w0[l], 'rw_w1': rw_w1[l], 'rw_w2': rw_w2[l],
            'rw_a0': rw_a0[l], 'rw_a1': rw_a1[l], 'rw_a2': rw_a2[l], 'rw_kk': rw_kk[l],
            'rw_ka': rw_ka[l], 'rw_rk': rw_rk[l], 'rw_ln_g': rw_ln_g[l], 'rw_ln_b': rw_ln_b[l],
            'rt_group_w': rt_group_w[l], 'rt_group_b': rt_group_b[l], 'rt_exp_w': rt_exp_w[l],
            'rt_exp_b': rt_exp_b[l], 'exp_w1': exp_w1[l], 'exp_w3': exp_w3[l], 'exp_w2': exp_w2[l],
        }
        x_p, k_l, v_l, s_l = trunk_layer(x_p, cond_p, lw, None, None, None)
        keys_p.append(k_l)
        vals_p.append(v_l)
        states_p.append(s_l.astype(x_prompt.dtype))
        x_s = trunk_layer(x_s, c, lw, cache_k[:, l], cache_v[:, l], state_wkv[:, l])
    y_prompt = rms_norm(x_p, norm_out)
    y_sample = rms_norm(x_s, norm_out)
    new_cache_k = jnp.stack(keys_p, axis=1)
    new_cache_v = jnp.stack(vals_p, axis=1)
    new_state_wkv = jnp.stack(states_p, axis=1)
    return (y_prompt, y_sample, new_cache_k, new_cache_v, new_state_wkv)
```

```python
import functools

import jax
import jax.numpy as jnp
from jax import lax
from jax.experimental import pallas as pl
from jax.experimental.pallas import tpu as pltpu

F32 = jnp.float32
BF16 = jnp.bfloat16
HIGHEST = lax.Precision.HIGHEST

D_MODEL = 1024
BATCH = 16
SEQ = 256
DEPTH = 4
DEC_BATCH = 4
DEC_SEQ = 2048
PAST_LEN = 256
GRID_W = 64
HEAD_DIM = 64
CONV_CH = 256
ATT_WIDTH = 512
ATT_HEADS = 8
ATT_KV_HEADS = 2
ATT_GQ = 4
KV_WIDTH = 128
WINDOW = 128
ATT_BLOCK = 128
RW_WIDTH = 256
RW_HEADS = 4
LORA = 64
IN_WIDTH = 2816
N_GROUPS = 4
EXP_PER_GROUP = 8
N_EXPERTS = 32
TOP_K = 2
D_EXPERT = 512
ROPE_BASE = 10000.0
NORM_EPS = 1e-6
GN_EPS = 64e-5
DECAY_SCALE = 0.6065306597126334
NEG_INF = -1e30

LANES = 128
SUBLANES = 8
N_CTX_ROWS = BATCH * SEQ
N_LAT_ROWS = DEC_BATCH * DEC_SEQ
N_ROWS = N_CTX_ROWS + N_LAT_ROWS
TM = 256
NT = N_ROWS // TM
NT_CTX = N_CTX_ROWS // TM
TILES_PER_LAT = DEC_SEQ // TM
N_COND = 8
D_CHUNKS = D_MODEL // LANES
MOE_BLOCK = 128
N_ASG = N_ROWS * TOP_K
N_MOE_BLOCKS = N_ASG // MOE_BLOCK + N_EXPERTS
N_SLOTS = N_MOE_BLOCKS * MOE_BLOCK
VMEM_LIMIT = 56 * 1024 * 1024


def _params(sem, vmem=VMEM_LIMIT):
    return pltpu.CompilerParams(dimension_semantics=sem, vmem_limit_bytes=vmem)


def _cond_of_tile(i):
    return jnp.where(i < NT_CTX, 0, 1 + (i - NT_CTX) // TILES_PER_LAT)


def _mod_spec(layer, j):
    return pl.BlockSpec((1, 1, D_MODEL),
                        lambda i: ((layer * N_COND + _cond_of_tile(i)) * 6 + j, 0, 0))


def _layer_spec(shape, layer):
    nd = len(shape)
    return pl.BlockSpec((None,) + tuple(shape), lambda *_: (layer,) + (0,) * nd)


def _rms(x, g):
    return x * lax.rsqrt(jnp.mean(x * x, axis=-1, keepdims=True) + NORM_EPS) * g


def _bdot(a, b):
    return jnp.dot(a.astype(BF16), b.astype(BF16), preferred_element_type=F32)


def _mod_kernel(c_ref, w_ref, b_ref, o_ref):
    c = c_ref[...]
    s = c * jax.nn.sigmoid(c)
    o_ref[...] = jnp.dot(s, w_ref[...], preferred_element_type=F32, precision=HIGHEST) + b_ref[...]


def _modulation(cond, w_mod, b_mod):
    tn = 1536
    return pl.pallas_call(
        _mod_kernel,
        grid=(DEPTH, 6 * D_MODEL // tn),
        in_specs=[pl.BlockSpec((N_COND, D_MODEL), lambda l, j: (0, 0)),
                  pl.BlockSpec((None, D_MODEL, tn), lambda l, j: (l, 0, j)),
                  pl.BlockSpec((None, 1, tn), lambda l, j: (l, 0, j))],
        out_specs=pl.BlockSpec((None, N_COND, tn), lambda l, j: (l, 0, j)),
        out_shape=jax.ShapeDtypeStruct((DEPTH, N_COND, 6 * D_MODEL), F32),
        compiler_params=_params(("arbitrary", "arbitrary")),
        name="modulation",
    )(cond, w_mod, b_mod.reshape(DEPTH, 1, 6 * D_MODEL))


def _inproj_kernel(x_ref, sh_ref, sc_ref, g_ref, w_ref, conv_ref, q_ref, k_ref, v_ref, rw_ref):
    h = _rms(x_ref[...], g_ref[...]) * (1.0 + sc_ref[0]) + sh_ref[0]
    p = jnp.dot(h.astype(BF16), w_ref[...], preferred_element_type=F32)
    conv_ref[...] = p[:, 0:768]
    q_ref[...] = p[:, 768:1280]
    k_ref[...] = p[:, 1280:1408]
    v_ref[...] = p[:, 1408:1536]
    rw_ref[...] = p[:, 1536:2816]


def _inproj(x, mod3, norm_mix, w_in_bf, layer):
    widths = (768, ATT_WIDTH, KV_WIDTH, KV_WIDTH, 5 * RW_WIDTH)
    return pl.pallas_call(
        _inproj_kernel,
        grid=(NT,),
        in_specs=[pl.BlockSpec((TM, D_MODEL), lambda i: (i, 0)),
                  _mod_spec(layer, 0), _mod_spec(layer, 1),
                  _layer_spec((1, D_MODEL), layer),
                  _layer_spec((D_MODEL, IN_WIDTH), layer)],
        out_specs=[pl.BlockSpec((TM, w), lambda i: (i, 0)) for w in widths],
        out_shape=[jax.ShapeDtypeStruct((N_ROWS, w), F32) for w in widths],
        compiler_params=_params(("parallel",)),
        name="inproj",
    )(x, mod3, mod3, norm_mix.reshape(DEPTH, 1, D_MODEL), w_in_bf)


def _tile_edges(a):
    c = a.shape[-1]
    a3 = a.reshape(NT, TM, c)
    first, last = a3[:, 0], a3[:, TM - 1]
    t = jnp.arange(NT)
    lat = t >= NT_CTX
    pos = (t - NT_CTX) % TILES_PER_LAT
    has_prev = (lat & (pos != 0))[:, None]
    has_next = (lat & (pos != TILES_PER_LAT - 1))[:, None]
    prev = jnp.where(has_prev, jnp.roll(last, 1, axis=0), 0.0)
    nxt = jnp.where(has_next, jnp.roll(first, -1, axis=0), 0.0)
    return prev.reshape(NT, 1, c), nxt.reshape(NT, 1, c)


def _shift_rows(x, prev_row, next_row):
    n = x.shape[0]
    row = lax.broadcasted_iota(jnp.int32, x.shape, 0)
    before = jnp.where(row == 0, prev_row, pltpu.roll(x, 1, axis=0))
    after = jnp.where(row == n - 1, next_row, pltpu.roll(x, n - 1, axis=0))
    return before, after


def _head_ones(n):
    r = lax.broadcasted_iota(jnp.int32, (n, n), 0) // HEAD_DIM
    c = lax.broadcasted_iota(jnp.int32, (n, n), 1) // HEAD_DIM
    return (r == c).astype(F32)


def _head_sum(x, ones):
    return jnp.dot(x, ones, preferred_element_type=F32, precision=HIGHEST)


def _rwpre_kernel(x_ref, pe_ref, ne_ref, mu_ref, w0_ref, w1_ref, w2_ref, a0_ref, a1_ref, a2_ref,
                  kk_ref, ka_ref, rk_ref, sc_ref, bonus_ref):
    x = x_ref[...]
    before, after = _shift_rows(x, pe_ref[0], ne_ref[0])
    ones = _head_ones(RW_WIDTH)
    bonus = jnp.zeros((TM, RW_WIDTH), F32)
    for d in range(2):
        nbr = before if d == 0 else after
        xd = x + (nbr - x) * mu_ref[d:d + 1, :]
        r = xd[:, 0:256]
        k = xd[:, 256:512]
        v = xd[:, 512:768]
        z = xd[:, 768:1024]
        lw = _bdot(jnp.tanh(_bdot(z, w1_ref[d])), w2_ref[d])
        w = jnp.exp(-DECAY_SCALE * jax.nn.sigmoid(w0_ref[d:d + 1, :] + lw))
        la = _bdot(_bdot(z, a1_ref[d]), a2_ref[d])
        a = jax.nn.sigmoid(a0_ref[d:d + 1, :] + la)
        kk = k * kk_ref[d:d + 1, :]
        kk = kk / jnp.maximum(jnp.sqrt(_head_sum(kk * kk, ones)), 1e-12)
        k2 = k * (1.0 + (a - 1.0) * ka_ref[d:d + 1, :])
        bonus = bonus + _head_sum(r * k2 * rk_ref[d:d + 1, :], ones) * v
        sc_ref[d, 0] = r
        sc_ref[d, 1] = w
        sc_ref[d, 2] = k2
        sc_ref[d, 3] = v
        sc_ref[d, 4] = -kk
        sc_ref[d, 5] = kk * a
    bonus_ref[...] = bonus


def _rwkv_pre(rw, prev_e, next_e, prm, layer, tile0, ntiles):
    (mu, w0, w1, w2, a0, a1, a2, kk, ka, rk) = prm
    nrows = ntiles * TM
    return pl.pallas_call(
        _rwpre_kernel,
        grid=(ntiles,),
        in_specs=[pl.BlockSpec((TM, 4 * RW_WIDTH), lambda i: (i + tile0, 0)),
                  pl.BlockSpec((1, 1, 4 * RW_WIDTH), lambda i: (i + tile0, 0, 0)),
                  pl.BlockSpec((1, 1, 4 * RW_WIDTH), lambda i: (i + tile0, 0, 0)),
                  _layer_spec((2, 4 * RW_WIDTH), layer),
                  _layer_spec((2, RW_WIDTH), layer),
                  _layer_spec((2, RW_WIDTH, LORA), layer),
                  _layer_spec((2, LORA, RW_WIDTH), layer),
                  _layer_spec((2, RW_WIDTH), layer),
                  _layer_spec((2, RW_WIDTH, LORA), layer),
                  _layer_spec((2, LORA, RW_WIDTH), layer),
                  _layer_spec((2, RW_WIDTH), layer),
                  _layer_spec((2, RW_WIDTH), layer),
                  _layer_spec((2, RW_WIDTH), layer)],
        out_specs=[pl.BlockSpec((2, 6, TM, RW_WIDTH), lambda i: (0, 0, i, 0)),
                   pl.BlockSpec((TM, RW_WIDTH), lambda i: (i, 0))],
        out_shape=[jax.ShapeDtypeStruct((2, 6, nrows, RW_WIDTH), F32),
                   jax.ShapeDtypeStruct((nrows, RW_WIDTH), F32)],
        compiler_params=_params(("parallel",)),
        name="rwkv_pre",
    )(rw, prev_e, next_e, mu, w0, w1, w2, a0, a1, a2, kk, ka, rk)


def _half_ones():
    r = lax.broadcasted_iota(jnp.int32, (LANES, LANES), 0) // HEAD_DIM
    c = lax.broadcasted_iota(jnp.int32, (LANES, LANES), 1) // HEAD_DIM
    return (r == c).astype(BF16)


def _split_dot(x, ones, parts):
    acc = None
    rem = x
    for p in range(parts):
        hi = rem.astype(BF16)
        y = jnp.dot(hi, ones, preferred_element_type=F32)
        acc = y if acc is None else acc + y
        if p + 1 < parts:
            rem = rem - hi.astype(F32)
    return acc


def _scan_kernel(scf_ref, scb_ref, s0_ref, yf_ref, yb_ref, sfin_ref, st_ref, *, nb, tblk, zero_init):
    tb = pl.program_id(1)
    ntb = pl.num_programs(1)

    @pl.when(tb == 0)
    def _():
        if zero_init:
            st_ref[...] = jnp.zeros_like(st_ref)
        else:
            st_ref[...] = s0_ref[...]

    ones = _half_ones()
    vi = lax.broadcasted_iota(jnp.int32, (HEAD_DIM, LANES), 0)
    li = lax.broadcasted_iota(jnp.int32, (HEAD_DIM, LANES), 1)
    diag = vi == (li % HEAD_DIM)

    def step(t, carry):
        for b in range(nb):
            for d in range(2):
                ref = scf_ref if d == 0 else scb_ref
                y_ref = yf_ref if d == 0 else yb_ref
                tt = t if d == 0 else tblk - 1 - t
                rows = [ref[j, b, pl.ds(tt, 1), :] for j in range(6)]
                ys = []
                for p in range(2):
                    r_t, w_t, k_t, v_t, a_t, b_t = [x[:, p * LANES:(p + 1) * LANES] for x in rows]
                    s = st_ref[b, d, p]
                    sa = _split_dot(s * a_t, ones, 2)
                    vcol = _split_dot(jnp.where(diag, v_t, 0.0), ones, 3)
                    s = s * w_t + sa * b_t + vcol * k_t
                    st_ref[b, d, p] = s
                    ybc = _split_dot(s * r_t, ones, 2)
                    ys.append(jnp.sum(jnp.where(diag, ybc, 0.0), axis=0, keepdims=True))
                y_ref[b, pl.ds(tt, 1), :] = jnp.concatenate(ys, axis=1)
        return carry

    lax.fori_loop(0, tblk, step, 0)

    @pl.when(tb == ntb - 1)
    def _():
        sfin_ref[...] = st_ref[...]


def _rwkv_scan(sc, s0, nseq, seqlen, nb, tblk, zero_init):
    ntb = seqlen // tblk
    kern = functools.partial(_scan_kernel, nb=nb, tblk=tblk, zero_init=zero_init)
    return pl.pallas_call(
        kern,
        grid=(nseq // nb, ntb),
        in_specs=[pl.BlockSpec((None, 6, nb, tblk, RW_WIDTH), lambda g, t: (0, 0, g, t, 0)),
                  pl.BlockSpec((None, 6, nb, tblk, RW_WIDTH), lambda g, t: (1, 0, g, ntb - 1 - t, 0)),
                  pl.BlockSpec((nb, 2, 2, HEAD_DIM, LANES), lambda g, t: (g, 0, 0, 0, 0))],
        out_specs=[pl.BlockSpec((nb, tblk, RW_WIDTH), lambda g, t: (g, t, 0)),
                   pl.BlockSpec((nb, tblk, RW_WIDTH), lambda g, t: (g, ntb - 1 - t, 0)),
                   pl.BlockSpec((nb, 2, 2, HEAD_DIM, LANES), lambda g, t: (g, 0, 0, 0, 0))],
        out_shape=[jax.ShapeDtypeStruct((nseq, seqlen, RW_WIDTH), F32),
                   jax.ShapeDtypeStruct((nseq, seqlen, RW_WIDTH), F32),
                   jax.ShapeDtypeStruct((nseq, 2, 2, HEAD_DIM, LANES), F32)],
        scratch_shapes=[pltpu.VMEM((nb, 2, 2, HEAD_DIM, LANES), F32)],
        compiler_params=_params(("parallel", "arbitrary")),
        name="rwkv_scan",
    )(sc, sc, s0)


def _state_to_pairs(s):
    b = s.shape[0]
    return s.reshape(b, 2, 2, 2, HEAD_DIM, HEAD_DIM).transpose(0, 1, 2, 4, 3, 5).reshape(b, 2, 2, HEAD_DIM, LANES)


def _pairs_to_state(s):
    b = s.shape[0]
    return s.reshape(b, 2, 2, HEAD_DIM, 2, HEAD_DIM).transpose(0, 1, 2, 4, 3, 5).reshape(b, 2, RW_HEADS, HEAD_DIM, HEAD_DIM)


def _rope(x, cos, sin):
    n = x.shape[1]
    m = n // LANES
    if m > 1:
        cos = jnp.concatenate([cos] * m, axis=1)
        sin = jnp.concatenate([sin] * m, axis=1)
    lane = lax.broadcasted_iota(jnp.int32, x.shape, 1)
    swapped = jnp.where((lane % 32) < 16, pltpu.roll(x, n - 16, axis=1), pltpu.roll(x, 16, axis=1))
    return x * cos + swapped * sin


def _attend(q, keys, vals, masks, sink_ref, o_ref):
    scale = HEAD_DIM ** -0.5
    qb = q.astype(BF16)
    kbs = [k.astype(BF16) for k in keys]
    vbs = [v.astype(BF16) for v in vals]
    for h in range(ATT_HEADS):
        hk = h // ATT_GQ
        qh = qb[:, h * HEAD_DIM:(h + 1) * HEAD_DIM]
        ss = []
        for kb, mk in zip(kbs, masks):
            s = lax.dot_general(qh, kb[:, hk * HEAD_DIM:(hk + 1) * HEAD_DIM],
                                (((1,), (1,)), ((), ())), preferred_element_type=F32) * scale
            if mk is not None:
                s = jnp.where(mk, s, NEG_INF)
            ss.append(s)
        sink = sink_ref[h:h + 1, 0:1]
        m = sink
        for s in ss:
            m = jnp.maximum(m, jnp.max(s, axis=-1, keepdims=True))
        den = jnp.exp(sink - m)
        acc = jnp.zeros((q.shape[0], HEAD_DIM), F32)
        for s, vb in zip(ss, vbs):
            p = jnp.exp(s - m)
            den = den + jnp.sum(p, axis=-1, keepdims=True)
            acc = acc + jnp.dot(p.astype(BF16), vb[:, hk * HEAD_DIM:(hk + 1) * HEAD_DIM],
                                preferred_element_type=F32)
        o_ref[:, h * HEAD_DIM:(h + 1) * HEAD_DIM] = acc / den


def _ctx_attn_kernel(q_ref, k_ref, v_ref, sink_ref, o_ref):
    _attend(q_ref[...], [k_ref[...]], [v_ref[...]], [None], sink_ref, o_ref)


def _ctx_attention(q, k, v, sink_b, layer):
    nq = SEQ // ATT_BLOCK
    return pl.pallas_call(
        _ctx_attn_kernel,
        grid=(BATCH, nq),
        in_specs=[pl.BlockSpec((ATT_BLOCK, ATT_WIDTH), lambda b, i: (b * nq + i, 0)),
                  pl.BlockSpec((SEQ, KV_WIDTH), lambda b, i: (b, 0)),
                  pl.BlockSpec((SEQ, KV_WIDTH), lambda b, i: (b, 0)),
                  pl.BlockSpec((None, ATT_HEADS, LANES), lambda b, i: (layer, 0, 0))],
        out_specs=pl.BlockSpec((ATT_BLOCK, ATT_WIDTH), lambda b, i: (b * nq + i, 0)),
        out_shape=jax.ShapeDtypeStruct((N_CTX_ROWS, ATT_WIDTH), F32),
        compiler_params=_params(("parallel", "parallel")),
        name="ctx_attention",
    )(q, k, v, sink_b)


def _lat_attn_kernel(q_ref, k_ref, v_ref, ck_ref, cv_ref, cos_ref, sin_ref, sink_ref, o_ref):
    n = pl.program_id(1)
    nb = pl.num_programs(1)
    r0 = pl.multiple_of(n * ATT_BLOCK, ATT_BLOCK)
    q = _rope(q_ref[...], cos_ref[pl.ds(r0, ATT_BLOCK), :], sin_ref[pl.ds(r0, ATT_BLOCK), :])
    qi = lax.broadcasted_iota(jnp.int32, (ATT_BLOCK, ATT_BLOCK), 0)
    ki = lax.broadcasted_iota(jnp.int32, (ATT_BLOCK, ATT_BLOCK), 1)
    keys, vals, masks = [], [], []
    for j in (-1, 0, 1):
        blk = jnp.clip(n + j, 0, nb - 1)
        k0 = pl.multiple_of(blk * ATT_BLOCK, ATT_BLOCK)
        kb = _rope(k_ref[pl.ds(k0, ATT_BLOCK), :], cos_ref[pl.ds(k0, ATT_BLOCK), :],
                   sin_ref[pl.ds(k0, ATT_BLOCK), :])
        keys.append(kb)
        vals.append(v_ref[pl.ds(k0, ATT_BLOCK), :])
        rel = ki + j * ATT_BLOCK - qi
        reach = jnp.where((n + j >= 0) & (n + j < nb), WINDOW, -1)
        masks.append(jnp.abs(rel) <= reach)
    keys.append(ck_ref[...])
    vals.append(cv_ref[...])
    masks.append(None)
    _attend(q, keys, vals, masks, sink_ref, o_ref)


def _lat_attention(q, k, v, cache_k4, cache_v4, cos, sin, sink_b, layer):
    nb = DEC_SEQ // ATT_BLOCK
    qoff = N_CTX_ROWS // ATT_BLOCK
    soff = N_CTX_ROWS // DEC_SEQ
    return pl.pallas_call(
        _lat_attn_kernel,
        grid=(DEC_BATCH, nb),
        in_specs=[pl.BlockSpec((ATT_BLOCK, ATT_WIDTH), lambda b, i: (qoff + b * nb + i, 0)),
                  pl.BlockSpec((DEC_SEQ, KV_WIDTH), lambda b, i: (soff + b, 0)),
                  pl.BlockSpec((DEC_SEQ, KV_WIDTH), lambda b, i: (soff + b, 0)),
                  pl.BlockSpec((None, None, PAST_LEN, KV_WIDTH), lambda b, i: (b, layer, 0, 0)),
                  pl.BlockSpec((None, None, PAST_LEN, KV_WIDTH), lambda b, i: (b, layer, 0, 0)),
                  pl.BlockSpec((DEC_SEQ, LANES), lambda b, i: (0, 0)),
                  pl.BlockSpec((DEC_SEQ, LANES), lambda b, i: (0, 0)),
                  pl.BlockSpec((None, ATT_HEADS, LANES), lambda b, i: (layer, 0, 0))],
        out_specs=pl.BlockSpec((ATT_BLOCK, ATT_WIDTH), lambda b, i: (b * nb + i, 0)),
        out_shape=jax.ShapeDtypeStruct((N_LAT_ROWS, ATT_WIDTH), F32),
        compiler_params=_params(("parallel", "parallel")),
        name="lat_attention",
    )(q, k, v, cache_k4, cache_v4, cos, sin, sink_b)


def _rope_tables():
    pos = jnp.arange(DEC_SEQ)
    rows = (pos // GRID_W).astype(F32)
    cols = (pos % GRID_W).astype(F32)
    inv = ROPE_BASE ** (-jnp.arange(0, 32, 2, dtype=F32) / 32)
    lane = jnp.arange(LANES)
    posl = jnp.where(((lane % HEAD_DIM) // 32 == 0)[None, :], rows[:, None], cols[:, None])
    ang = posl * inv[lane % 16][None, :]
    sign = jnp.where((lane % 32) < 16, -1.0, 1.0)[None, :]
    return jnp.cos(ang), jnp.sin(ang) * sign


def _outproj_kernel(x_ref, conv_ref, cpe_ref, cne_ref, att_ref, yf_ref, yb_ref, bonus_ref, g_ref,
                    cw_ref, lng_ref, lnb_ref, wout_ref, gate_ref, sh_ref, sc_ref, nf_ref, rtw_ref, rtb_ref,
                    xo_ref, h3_ref, route_ref):
    conv = conv_ref[...]
    u = conv[:, 256:512] * conv[:, 512:768]
    pe = cpe_ref[0]
    ne = cne_ref[0]
    before, after = _shift_rows(u, pe[:, 256:512] * pe[:, 512:768], ne[:, 256:512] * ne[:, 512:768])
    y_conv = conv[:, 0:256] * (before * cw_ref[0:1, :] + u * cw_ref[1:2, :] + after * cw_ref[2:3, :])

    ones = _head_ones(RW_WIDTH)
    inv_hd = 1.0 / HEAD_DIM
    ysum = yf_ref[...] + yb_ref[...]
    mean = _head_sum(ysum, ones) * inv_hd
    cen = ysum - mean
    var = _head_sum(cen * cen, ones) * inv_hd
    yn = cen * lax.rsqrt(var + GN_EPS) * lng_ref[...] + lnb_ref[...]
    y_rw = (yn + bonus_ref[...]) * jax.nn.sigmoid(g_ref[...])

    mix = (jnp.dot(y_conv.astype(BF16), wout_ref[0:256, :], preferred_element_type=F32)
           + jnp.dot(att_ref[...].astype(BF16), wout_ref[256:768, :], preferred_element_type=F32)
           + jnp.dot(y_rw.astype(BF16), wout_ref[768:1024, :], preferred_element_type=F32))
    x = x_ref[...] + gate_ref[0] * mix
    xo_ref[...] = x
    h2 = _rms(x, nf_ref[...]) * (1.0 + sc_ref[0]) + sh_ref[0]
    for j in range(D_CHUNKS):
        h3_ref[:, j, :] = h2[:, j * LANES:(j + 1) * LANES]

    logits = jnp.dot(h2, rtw_ref[...], preferred_element_type=F32, precision=HIGHEST) + rtb_ref[...]
    lane = lax.broadcasted_iota(jnp.int32, logits.shape, 1)
    big = jnp.int32(1 << 20)
    is_g = lane < N_GROUPS
    gmax = jnp.max(jnp.where(is_g, logits, NEG_INF), axis=-1, keepdims=True)
    gsum = jnp.sum(jnp.where(is_g, jnp.exp(logits - gmax), 0.0), axis=-1, keepdims=True)
    g_top = 1.0 / gsum
    g_idx = jnp.min(jnp.where(is_g & (logits == gmax), lane, big), axis=-1, keepdims=True)
    lo = N_GROUPS + g_idx * EXP_PER_GROUP
    in_grp = (lane >= lo) & (lane < lo + EXP_PER_GROUP)
    emax = jnp.max(jnp.where(in_grp, logits, NEG_INF), axis=-1, keepdims=True)
    pe_ = jnp.where(in_grp, jnp.exp(logits - emax), -1.0)
    p1 = jnp.max(pe_, axis=-1, keepdims=True)
    l1 = jnp.min(jnp.where(pe_ == p1, lane, big), axis=-1, keepdims=True)
    pe2 = jnp.where(lane == l1, -1.0, pe_)
    p2 = jnp.max(pe2, axis=-1, keepdims=True)
    l2 = jnp.min(jnp.where(pe2 == p2, lane, big), axis=-1, keepdims=True)
    wsum = p1 + p2
    w1 = g_top * p1 / wsum
    w2 = g_top * p2 / wsum
    e1 = (l1 - N_GROUPS).astype(F32)
    e2 = (l2 - N_GROUPS).astype(F32)
    route_ref[...] = jnp.where(lane == 0, e1, jnp.where(lane == 1, e2, jnp.where(lane == 2, w1,
                               jnp.where(lane == 3, w2, 0.0))))


def _outproj(x, conv, cpe, cne, att, yf, yb, bonus, rw, mod3, prm, layer):
    conv_w, ln_g, ln_b, w_out_bf, norm_ffn, rt_w, rt_b = prm
    row = lambda w: pl.BlockSpec((TM, w), lambda i: (i, 0))
    edge = pl.BlockSpec((1, 1, 768), lambda i: (i, 0, 0))
    return pl.pallas_call(
        _outproj_kernel,
        grid=(NT,),
        in_specs=[row(D_MODEL), row(768), edge, edge, row(ATT_WIDTH), row(RW_WIDTH), row(RW_WIDTH),
                  row(RW_WIDTH), pl.BlockSpec((TM, RW_WIDTH), lambda i: (i, 4)),
                  _layer_spec((3, CONV_CH), layer), _layer_spec((1, RW_WIDTH), layer),
                  _layer_spec((1, RW_WIDTH), layer), _layer_spec((D_MODEL, D_MODEL), layer),
                  _mod_spec(layer, 2), _mod_spec(layer, 3), _mod_spec(layer, 4),
                  _layer_spec((1, D_MODEL), layer), _layer_spec((D_MODEL, LANES), layer),
                  _layer_spec((1, LANES), layer)],
        out_specs=[row(D_MODEL), pl.BlockSpec((TM, D_CHUNKS, LANES), lambda i: (i, 0, 0)), row(LANES)],
        out_shape=[jax.ShapeDtypeStruct((N_ROWS, D_MODEL), F32),
                   jax.ShapeDtypeStruct((N_ROWS, D_CHUNKS, LANES), F32),
                   jax.ShapeDtypeStruct((N_ROWS, LANES), F32)],
        compiler_params=_params(("parallel",)),
        name="outproj_router",
    )(x, conv, cpe, cne, att, yf, yb, bonus, rw, conv_w, ln_g, ln_b, w_out_bf, mod3, mod3, mod3,
      norm_ffn, rt_w, rt_b)


def _slot_plan(route):
    e = route[:, 0:2].astype(jnp.int32)
    oh = (e[:, 0:1] == jnp.arange(N_EXPERTS)[None, :]).astype(jnp.int32) + \
         (e[:, 1:2] == jnp.arange(N_EXPERTS)[None, :]).astype(jnp.int32)
    csum = jnp.cumsum(oh, axis=0)
    counts = csum[-1]
    rank = jnp.take_along_axis(csum - oh, e, axis=1)
    padded = (counts + MOE_BLOCK - 1) // MOE_BLOCK * MOE_BLOCK
    pad_end = jnp.cumsum(padded)
    pad_start = pad_end - padded
    slots = (pad_start[e] + rank).astype(jnp.int32)
    blk0 = jnp.arange(N_MOE_BLOCKS, dtype=jnp.int32) * MOE_BLOCK
    block_exp = jnp.minimum(jnp.searchsorted(pad_end, blk0, side='right'), N_EXPERTS - 1).astype(jnp.int32)
    n_used = (pad_end[-1] // MOE_BLOCK).astype(jnp.int32)
    return slots.reshape(-1), block_exp, n_used.reshape(1)


def _dispatch_kernel(slots_ref, h3_ref, xs_in_ref, xs_ref, sem):
    del xs_in_ref
    i = pl.program_id(0)
    base = i * (TM * TOP_K)

    def issue(r, c):
        for k in range(TOP_K):
            s = slots_ref[base + r * TOP_K + k]
            pltpu.make_async_copy(h3_ref.at[pl.ds(r, 1)], xs_ref.at[pl.ds(s, 1)], sem).start()
        return c

    lax.fori_loop(0, TM, issue, 0)

    def drain(r, c):
        for k in range(TOP_K):
            pltpu.make_async_copy(h3_ref.at[pl.ds(0, 1)], xs_ref.at[pl.ds(0, 1)], sem).wait()
        return c

    lax.fori_loop(0, TM, drain, 0)


def _dispatch(slots, h3, xs_zero):
    return pl.pallas_call(
        _dispatch_kernel,
        grid_spec=pltpu.PrefetchScalarGridSpec(
            num_scalar_prefetch=1,
            grid=(NT,),
            in_specs=[pl.BlockSpec((TM, D_CHUNKS, LANES), lambda i, s: (i, 0, 0)),
                      pl.BlockSpec(memory_space=pl.ANY)],
            out_specs=pl.BlockSpec(memory_space=pl.ANY),
            scratch_shapes=[pltpu.SemaphoreType.DMA(())]),
        out_shape=jax.ShapeDtypeStruct((N_SLOTS, D_CHUNKS, LANES), F32),
        input_output_aliases={2: 0},
        compiler_params=_params(("arbitrary",)),
        name="moe_dispatch",
    )(slots, h3, xs_zero)


def _expert_kernel(be_ref, nu_ref, xs_ref, w1_ref, w3_ref, w2_ref, ys_ref):
    i = pl.program_id(0)

    @pl.when(i < nu_ref[0])
    def _():
        a = jnp.zeros((MOE_BLOCK, D_EXPERT), F32)
        b = jnp.zeros((MOE_BLOCK, D_EXPERT), F32)
        for j in range(D_CHUNKS):
            xj = xs_ref[:, j, :].astype(BF16)
            a = a + jnp.dot(xj, w1_ref[j * LANES:(j + 1) * LANES, :].astype(BF16), preferred_element_type=F32)
            b = b + jnp.dot(xj, w3_ref[j * LANES:(j + 1) * LANES, :].astype(BF16), preferred_element_type=F32)
        hid = (a * jax.nn.sigmoid(a) * b).astype(BF16)
        w2 = w2_ref[...].astype(BF16)
        for j in range(D_CHUNKS):
            ys_ref[:, j, :] = jnp.dot(hid, w2[:, j * LANES:(j + 1) * LANES], preferred_element_type=F32)

    @pl.when(i >= nu_ref[0])
    def _():
        ys_ref[...] = jnp.zeros_like(ys_ref)


def _experts(block_exp, n_used, xs, w1, w3, w2, layer):
    def wmap(i, be, nu):
        return (layer, be[jnp.minimum(i, nu[0] - 1)], 0, 0)

    return pl.pallas_call(
        _expert_kernel,
        grid_spec=pltpu.PrefetchScalarGridSpec(
            num_scalar_prefetch=2,
            grid=(N_MOE_BLOCKS,),
            in_specs=[pl.BlockSpec((MOE_BLOCK, D_CHUNKS, LANES), lambda i, be, nu: (i, 0, 0)),
                      pl.BlockSpec((None, None, D_MODEL, D_EXPERT), wmap),
                      pl.BlockSpec((None, None, D_MODEL, D_EXPERT), wmap),
                      pl.BlockSpec((None, None, D_EXPERT, D_MODEL), wmap)],
            out_specs=pl.BlockSpec((MOE_BLOCK, D_CHUNKS, LANES), lambda i, be, nu: (i, 0, 0))),
        out_shape=jax.ShapeDtypeStruct((N_SLOTS, D_CHUNKS, LANES), F32),
        compiler_params=_params(("arbitrary",)),
        name="moe_experts",
    )(block_exp, n_used, xs, w1, w3, w2)


def _combine_kernel(slots_ref, x_ref, route_ref, gate_ref, g_ref, ys_ref, o_ref, buf, sem, *, final_norm):
    i = pl.program_id(0)
    base = i * (TM * TOP_K)

    def issue(r, c):
        for k in range(TOP_K):
            s = slots_ref[base + r * TOP_K + k]
            pltpu.make_async_copy(ys_ref.at[pl.ds(s, 1)], buf.at[k, pl.ds(r, 1)], sem).start()
        return c

    lax.fori_loop(0, TM, issue, 0)

    def drain(r, c):
        for k in range(TOP_K):
            pltpu.make_async_copy(ys_ref.at[pl.ds(0, 1)], buf.at[k, pl.ds(0, 1)], sem).wait()
        return c

    lax.fori_loop(0, TM, drain, 0)
    route = route_ref[...]
    w1 = route[:, 2:3]
    w2 = route[:, 3:4]
    y = jnp.concatenate([w1 * buf[0, :, j, :] + w2 * buf[1, :, j, :] for j in range(D_CHUNKS)], axis=1)
    x = x_ref[...] + gate_ref[0] * y
    if final_norm:
        x = _rms(x, g_ref[...])
    o_ref[...] = x


def _combine(slots, x, route, mod3, norm_out, ys, layer, final_norm):
    kern = functools.partial(_combine_kernel, final_norm=final_norm)
    return pl.pallas_call(
        kern,
        grid_spec=pltpu.PrefetchScalarGridSpec(
            num_scalar_prefetch=1,
            grid=(NT,),
            in_specs=[pl.BlockSpec((TM, D_MODEL), lambda i, s: (i, 0)),
                      pl.BlockSpec((TM, LANES), lambda i, s: (i, 0)),
                      pl.BlockSpec((1, 1, D_MODEL),
                                   lambda i, s: ((layer * N_COND + _cond_of_tile(i)) * 6 + 5, 0, 0)),
                      pl.BlockSpec((1, D_MODEL), lambda i, s: (0, 0)),
                      pl.BlockSpec(memory_space=pl.ANY)],
            out_specs=pl.BlockSpec((TM, D_MODEL), lambda i, s: (i, 0)),
            scratch_shapes=[pltpu.VMEM((TOP_K, TM, D_CHUNKS, LANES), F32),
                            pltpu.SemaphoreType.DMA(())]),
        out_shape=jax.ShapeDtypeStruct((N_ROWS, D_MODEL), F32),
        compiler_params=_params(("arbitrary",)),
        name="moe_combine",
    )(slots, x, route, mod3, norm_out.reshape(1, D_MODEL), ys)


def kernel(x_prompt, x_sample, cache_k, cache_v, state_wkv, c, c_ctx, w_mod, b_mod, norm_mix, norm_ffn,
           norm_out, w_in, w_out, conv_w, attn_sink, rw_mu, rw_w0, rw_w1, rw_w2, rw_a0, rw_a1, rw_a2,
           rw_kk, rw_ka, rw_rk, rw_ln_g, rw_ln_b, rt_group_w, rt_group_b, rt_exp_w, rt_exp_b,
           exp_w1, exp_w3, exp_w2):
    cond = jnp.concatenate([c_ctx[None, :], c, jnp.zeros((N_COND - 1 - DEC_BATCH, D_MODEL), F32)], axis=0)
    mod3 = _modulation(cond, w_mod, b_mod).reshape(DEPTH * N_COND * 6, 1, D_MODEL)

    w_in_bf = w_in.astype(BF16)
    w_out_bf = w_out.astype(BF16)
    pad = LANES - N_GROUPS - N_EXPERTS
    rt_w = jnp.concatenate([rt_group_w, rt_exp_w, jnp.zeros((DEPTH, D_MODEL, pad), F32)], axis=-1)
    rt_b = jnp.concatenate([rt_group_b, rt_exp_b, jnp.zeros((DEPTH, pad), F32)], axis=-1).reshape(DEPTH, 1, LANES)
    sink_b = jnp.broadcast_to(attn_sink[:, :, None], (DEPTH, ATT_HEADS, LANES))
    cos, sin = _rope_tables()
    cache_k4 = cache_k.reshape(DEC_BATCH, DEPTH, PAST_LEN, KV_WIDTH)
    cache_v4 = cache_v.reshape(DEC_BATCH, DEPTH, PAST_LEN, KV_WIDTH)
    s0_lat = _state_to_pairs(state_wkv.transpose(1, 0, 2, 3, 4, 5).reshape(DEPTH * DEC_BATCH, 2, RW_HEADS, HEAD_DIM, HEAD_DIM))
    s0_lat = s0_lat.reshape(DEPTH, DEC_BATCH, 2, 2, HEAD_DIM, LANES)
    s0_ctx = jnp.zeros((BATCH, 2, 2, HEAD_DIM, LANES), F32)
    xs_zero = jnp.zeros((N_SLOTS, D_CHUNKS, LANES), F32)
    rw_prm = (rw_mu, rw_w0, rw_w1, rw_w2, rw_a0, rw_a1, rw_a2, rw_kk, rw_ka, rw_rk)
    out_prm = (conv_w, rw_ln_g.reshape(DEPTH, 1, RW_WIDTH), rw_ln_b.reshape(DEPTH, 1, RW_WIDTH), w_out_bf,
               norm_ffn.reshape(DEPTH, 1, D_MODEL), rt_w, rt_b)

    x = jnp.concatenate([x_prompt.reshape(N_CTX_ROWS, D_MODEL), x_sample.reshape(N_LAT_ROWS, D_MODEL)], axis=0)
    keys, vals, states = [], [], []
    for l in range(DEPTH):
        conv, q, k, v, rw = _inproj(x, mod3, norm_mix, w_in_bf, l)
        cpe, cne = _tile_edges(conv)
        rpe, rne = _tile_edges(rw)
        rpe, rne = rpe[:, :, 0:4 * RW_WIDTH], rne[:, :, 0:4 * RW_WIDTH]

        sc_c, bonus_c = _rwkv_pre(rw, rpe, rne, rw_prm, l, 0, NT_CTX)
        sc_l, bonus_l = _rwkv_pre(rw, rpe, rne, rw_prm, l, NT_CTX, NT - NT_CTX)
        yf_c, yb_c, sfin = _rwkv_scan(sc_c.reshape(2, 6, BATCH, SEQ, RW_WIDTH), s0_ctx, BATCH, SEQ, 4, SEQ, True)
        yf_l, yb_l, _ = _rwkv_scan(sc_l.reshape(2, 6, DEC_BATCH, DEC_SEQ, RW_WIDTH), s0_lat[l], DEC_BATCH,
                                   DEC_SEQ, 4, 256, False)
        yf = jnp.concatenate([yf_c.reshape(N_CTX_ROWS, RW_WIDTH), yf_l.reshape(N_LAT_ROWS, RW_WIDTH)], axis=0)
        yb = jnp.concatenate([yb_c.reshape(N_CTX_ROWS, RW_WIDTH), yb_l.reshape(N_LAT_ROWS, RW_WIDTH)], axis=0)
        bonus = jnp.concatenate([bonus_c, bonus_l], axis=0)

        att_c = _ctx_attention(q, k, v, sink_b, l)
        att_l = _lat_attention(q, k, v, cache_k4, cache_v4, cos, sin, sink_b, l)
        att = jnp.concatenate([att_c, att_l], axis=0)

        x_mid, h3, route = _outproj(x, conv, cpe, cne, att, yf, yb, bonus, rw, mod3, out_prm, l)
        slots, block_exp, n_used = _slot_plan(route)
        xs = _dispatch(slots, h3, xs_zero)
        ys = _experts(block_exp, n_used, xs, exp_w1, exp_w3, exp_w2, l)
        x = _combine(slots, x_mid, route, mod3, norm_out, ys, l, l == DEPTH - 1)

        keys.append(k[:N_CTX_ROWS].reshape(BATCH, SEQ, ATT_KV_HEADS, HEAD_DIM))
        vals.append(v[:N_CTX_ROWS].reshape(BATCH, SEQ, ATT_KV_HEADS, HEAD_DIM))
        states.append(_pairs_to_state(sfin))

    y_prompt = x[:N_CTX_ROWS].reshape(BATCH, SEQ, D_MODEL)
    y_sample = x[N_CTX_ROWS:].reshape(DEC_BATCH, DEC_SEQ, D_MODEL)
    return (y_prompt, y_sample, jnp.stack(keys, axis=1), jnp.stack(vals, axis=1), jnp.stack(states, axis=1))
```

```python
import functools

import jax
import jax.numpy as jnp
from jax import lax
from jax.experimental import pallas as pl
from jax.experimental.pallas import tpu as pltpu

F32 = jnp.float32
BF16 = jnp.bfloat16
HIGHEST = lax.Precision.HIGHEST

D_MODEL = 1024
BATCH = 16
SEQ = 256
DEPTH = 4
DEC_BATCH = 4
DEC_SEQ = 2048
PAST_LEN = 256
GRID_W = 64
HEAD_DIM = 64
CONV_CH = 256
ATT_WIDTH = 512
ATT_HEADS = 8
ATT_KV_HEADS = 2
ATT_GQ = 4
KV_WIDTH = 128
WINDOW = 128
ATT_BLOCK = 128
RW_WIDTH = 256
RW_HEADS = 4
LORA = 64
IN_WIDTH = 2816
N_GROUPS = 4
EXP_PER_GROUP = 8
N_EXPERTS = 32
TOP_K = 2
D_EXPERT = 512
ROPE_BASE = 10000.0
NORM_EPS = 1e-6
GN_EPS = 64e-5
DECAY_SCALE = 0.6065306597126334
NEG_INF = -1e30

LANES = 128
SUBLANES = 8
N_CTX_ROWS = BATCH * SEQ
N_LAT_ROWS = DEC_BATCH * DEC_SEQ
N_ROWS = N_CTX_ROWS + N_LAT_ROWS
TM = 256
NT = N_ROWS // TM
NT_CTX = N_CTX_ROWS // TM
TILES_PER_LAT = DEC_SEQ // TM
N_COND = 8
D_CHUNKS = D_MODEL // LANES
MOE_BLOCK = 128
N_ASG = N_ROWS * TOP_K
N_MOE_BLOCKS = N_ASG // MOE_BLOCK + N_EXPERTS
N_SLOTS = N_MOE_BLOCKS * MOE_BLOCK
VMEM_LIMIT = 56 * 1024 * 1024


def _params(sem, vmem=VMEM_LIMIT):
    return pltpu.CompilerParams(dimension_semantics=sem, vmem_limit_bytes=vmem)


def _cond_of_tile(i):
    return jnp.where(i < NT_CTX, 0, 1 + (i - NT_CTX) // TILES_PER_LAT)


def _mod_spec(layer, j):
    return pl.BlockSpec((1, 1, D_MODEL),
                        lambda i: ((layer * N_COND + _cond_of_tile(i)) * 6 + j, 0, 0))


def _layer_spec(shape, layer):
    nd = len(shape)
    return pl.BlockSpec((None,) + tuple(shape), lambda *_: (layer,) + (0,) * nd)


def _rms(x, g):
    return x * lax.rsqrt(jnp.mean(x * x, axis=-1, keepdims=True) + NORM_EPS) * g


def _bdot(a, b):
    return jnp.dot(a.astype(BF16), b.astype(BF16), preferred_element_type=F32)


def _mod_kernel(c_ref, w_ref, b_ref, o_ref):
    c = c_ref[...]
    s = c * jax.nn.sigmoid(c)
    o_ref[...] = jnp.dot(s, w_ref[...], preferred_element_type=F32, precision=HIGHEST) + b_ref[...]


def _modulation(cond, w_mod, b_mod):
    tn = 1536
    return pl.pallas_call(
        _mod_kernel,
        grid=(DEPTH, 6 * D_MODEL // tn),
        in_specs=[pl.BlockSpec((N_COND, D_MODEL), lambda l, j: (0, 0)),
                  pl.BlockSpec((None, D_MODEL, tn), lambda l, j: (l, 0, j)),
                  pl.BlockSpec((None, 1, tn), lambda l, j: (l, 0, j))],
        out_specs=pl.BlockSpec((None, N_COND, tn), lambda l, j: (l, 0, j)),
        out_shape=jax.ShapeDtypeStruct((DEPTH, N_COND, 6 * D_MODEL), F32),
        compiler_params=_params(("arbitrary", "arbitrary")),
        name="modulation",
    )(cond, w_mod, b_mod.reshape(DEPTH, 1, 6 * D_MODEL))


def _inproj_kernel(x_ref, sh_ref, sc_ref, g_ref, w_ref, conv_ref, q_ref, k_ref, v_ref, rw_ref):
    h = _rms(x_ref[...], g_ref[...]) * (1.0 + sc_ref[0]) + sh_ref[0]
    p = jnp.dot(h.astype(BF16), w_ref[...], preferred_element_type=F32)
    conv_ref[...] = p[:, 0:768]
    q_ref[...] = p[:, 768:1280]
    k_ref[...] = p[:, 1280:1408]
    v_ref[...] = p[:, 1408:1536]
    rw_ref[...] = p[:, 1536:2816]


def _inproj(x, mod3, norm_mix, w_in_bf, layer):
    widths = (768, ATT_WIDTH, KV_WIDTH, KV_WIDTH, 5 * RW_WIDTH)
    return pl.pallas_call(
        _inproj_kernel,
        grid=(NT,),
        in_specs=[pl.BlockSpec((TM, D_MODEL), lambda i: (i, 0)),
                  _mod_spec(layer, 0), _mod_spec(layer, 1),
                  _layer_spec((1, D_MODEL), layer),
                  _layer_spec((D_MODEL, IN_WIDTH), layer)],
        out_specs=[pl.BlockSpec((TM, w), lambda i: (i, 0)) for w in widths],
        out_shape=[jax.ShapeDtypeStruct((N_ROWS, w), F32) for w in widths],
        compiler_params=_params(("parallel",)),
        name="inproj",
    )(x, mod3, mod3, norm_mix.reshape(DEPTH, 1, D_MODEL), w_in_bf)


def _tile_edges(a):
    c = a.shape[-1]
    a3 = a.reshape(NT, TM, c)
    first, last = a3[:, 0], a3[:, TM - 1]
    t = jnp.arange(NT)
    lat = t >= NT_CTX
    pos = (t - NT_CTX) % TILES_PER_LAT
    has_prev = (lat & (pos != 0))[:, None]
    has_next = (lat & (pos != TILES_PER_LAT - 1))[:, None]
    prev = jnp.where(has_prev, jnp.roll(last, 1, axis=0), 0.0)
    nxt = jnp.where(has_next, jnp.roll(first, -1, axis=0), 0.0)
    return prev.reshape(NT, 1, c), nxt.reshape(NT, 1, c)


def _shift_rows(x, prev_row, next_row):
    n = x.shape[0]
    row = lax.broadcasted_iota(jnp.int32, x.shape, 0)
    before = jnp.where(row == 0, prev_row, pltpu.roll(x, 1, axis=0))
    after = jnp.where(row == n - 1, next_row, pltpu.roll(x, n - 1, axis=0))
    return before, after


def _head_ones(n):
    r = lax.broadcasted_iota(jnp.int32, (n, n), 0) // HEAD_DIM
    c = lax.broadcasted_iota(jnp.int32, (n, n), 1) // HEAD_DIM
    return (r == c).astype(F32)


def _head_sum(x, ones):
    return jnp.dot(x, ones, preferred_element_type=F32, precision=HIGHEST)


def _rwpre_kernel(x_ref, pe_ref, ne_ref, mu_ref, w0_ref, w1_ref, w2_ref, a0_ref, a1_ref, a2_ref,
                  kk_ref, ka_ref, rk_ref, sc_ref, bonus_ref):
    x = x_ref[...]
    before, after = _shift_rows(x, pe_ref[0], ne_ref[0])
    ones = _head_ones(RW_WIDTH)
    bonus = jnp.zeros((TM, RW_WIDTH), F32)
    for d in range(2):
        nbr = before if d == 0 else after
        xd = x + (nbr - x) * mu_ref[d:d + 1, :]
        r = xd[:, 0:256]
        k = xd[:, 256:512]
        v = xd[:, 512:768]
        z = xd[:, 768:1024]
        lw = _bdot(jnp.tanh(_bdot(z, w1_ref[d])), w2_ref[d])
        w = jnp.exp(-DECAY_SCALE * jax.nn.sigmoid(w0_ref[d:d + 1, :] + lw))
        la = _bdot(_bdot(z, a1_ref[d]), a2_ref[d])
        a = jax.nn.sigmoid(a0_ref[d:d + 1, :] + la)
        kk = k * kk_ref[d:d + 1, :]
        kk = kk / jnp.maximum(jnp.sqrt(_head_sum(kk * kk, ones)), 1e-12)
        k2 = k * (1.0 + (a - 1.0) * ka_ref[d:d + 1, :])
        bonus = bonus + _head_sum(r * k2 * rk_ref[d:d + 1, :], ones) * v
        sc_ref[d, 0] = r
        sc_ref[d, 1] = w
        sc_ref[d, 2] = k2
        sc_ref[d, 3] = v
        sc_ref[d, 4] = -kk
        sc_ref[d, 5] = kk * a
    bonus_ref[...] = bonus


def _rwkv_pre(rw, prev_e, next_e, prm, layer, tile0, ntiles):
    (mu, w0, w1, w2, a0, a1, a2, kk, ka, rk) = prm
    nrows = ntiles * TM
    return pl.pallas_call(
        _rwpre_kernel,
        grid=(ntiles,),
        in_specs=[pl.BlockSpec((TM, 4 * RW_WIDTH), lambda i: (i + tile0, 0)),
                  pl.BlockSpec((1, 1, 4 * RW_WIDTH), lambda i: (i + tile0, 0, 0)),
                  pl.BlockSpec((1, 1, 4 * RW_WIDTH), lambda i: (i + tile0, 0, 0)),
                  _layer_spec((2, 4 * RW_WIDTH), layer),
                  _layer_spec((2, RW_WIDTH), layer),
                  _layer_spec((2, RW_WIDTH, LORA), layer),
                  _layer_spec((2, LORA, RW_WIDTH), layer),
                  _layer_spec((2, RW_WIDTH), layer),
                  _layer_spec((2, RW_WIDTH, LORA), layer),
                  _layer_spec((2, LORA, RW_WIDTH), layer),
                  _layer_spec((2, RW_WIDTH), layer),
                  _layer_spec((2, RW_WIDTH), layer),
                  _layer_spec((2, RW_WIDTH), layer)],
        out_specs=[pl.BlockSpec((2, 6, TM, RW_WIDTH), lambda i: (0, 0, i, 0)),
                   pl.BlockSpec((TM, RW_WIDTH), lambda i: (i, 0))],
        out_shape=[jax.ShapeDtypeStruct((2, 6, nrows, RW_WIDTH), F32),
                   jax.ShapeDtypeStruct((nrows, RW_WIDTH), F32)],
        compiler_params=_params(("parallel",)),
        name="rwkv_pre",
    )(rw, prev_e, next_e, mu, w0, w1, w2, a0, a1, a2, kk, ka, rk)


def _half_ones():
    r = lax.broadcasted_iota(jnp.int32, (LANES, LANES), 0) // HEAD_DIM
    c = lax.broadcasted_iota(jnp.int32, (LANES, LANES), 1) // HEAD_DIM
    return (r == c).astype(BF16)


def _scan_kernel(scf_ref, scb_ref, s0_ref, yf_ref, yb_ref, sfin_ref, st_ref, *, nb, tblk, zero_init):
    tb = pl.program_id(1)
    ntb = pl.num_programs(1)

    @pl.when(tb == 0)
    def _():
        if zero_init:
            st_ref[...] = jnp.zeros_like(st_ref)
        else:
            st_ref[...] = s0_ref[...]

    ones = _half_ones()
    vi = lax.broadcasted_iota(jnp.int32, (HEAD_DIM, LANES), 0)
    li = lax.broadcasted_iota(jnp.int32, (HEAD_DIM, LANES), 1)
    diag = vi == (li % HEAD_DIM)
    diag_bf = diag.astype(BF16)
    chains = [(b, d, p) for b in range(nb) for d in range(2) for p in range(2)]
    r_row, w_row, k_row, v_row, a_row, b_row = range(6)

    def row_of(t, d):
        return t if d == 0 else tblk - 1 - t

    def load_rows(t, which):
        rows = {}
        for b in range(nb):
            for d in range(2):
                ref = scf_ref if d == 0 else scb_ref
                full = {j: ref[j, b, pl.ds(row_of(t, d), 1), :] for j in which}
                for p in range(2):
                    rows[(b, d, p)] = {j: x[:, p * LANES:(p + 1) * LANES] for j, x in full.items()}
        return rows

    def states():
        return {c: st_ref[c[0], c[1], c[2]] for c in chains}

    def emit_y(t, ss, rows):
        pr = jnp.concatenate([(ss[c] * rows[c][r_row]).astype(BF16) for c in chains], axis=0)
        ybc = jnp.dot(pr, ones, preferred_element_type=F32)
        ys = {}
        for i, c in enumerate(chains):
            blk = ybc[i * HEAD_DIM:(i + 1) * HEAD_DIM]
            ys[c] = jnp.sum(jnp.where(diag, blk, 0.0), axis=0, keepdims=True)
        for b in range(nb):
            for d in range(2):
                y_ref = yf_ref if d == 0 else yb_ref
                y_ref[b, pl.ds(row_of(t, d), 1), :] = jnp.concatenate([ys[(b, d, 0)], ys[(b, d, 1)]], axis=1)

    def step(t, carry):
        rows = load_rows(t, (w_row, k_row, v_row, a_row, b_row))
        tp = jnp.maximum(t - 1, 0)
        ss = states()
        emit_y(tp, ss, load_rows(tp, (r_row,)))
        pa = jnp.concatenate([(ss[c] * rows[c][a_row]).astype(BF16) for c in chains], axis=0)
        dv = jnp.concatenate([diag_bf * rows[c][v_row].astype(BF16) for c in chains], axis=0)
        sa = jnp.dot(pa, ones, preferred_element_type=F32)
        vc = jnp.dot(dv, ones, preferred_element_type=F32)
        for i, c in enumerate(chains):
            rw = rows[c]
            sl = slice(i * HEAD_DIM, (i + 1) * HEAD_DIM)
            st_ref[c[0], c[1], c[2]] = ss[c] * rw[w_row] + sa[sl] * rw[b_row] + vc[sl] * rw[k_row]
        return carry

    lax.fori_loop(0, tblk, step, 0)
    emit_y(tblk - 1, states(), load_rows(tblk - 1, (r_row,)))

    @pl.when(tb == ntb - 1)
    def _():
        sfin_ref[...] = st_ref[...]


def _rwkv_scan(sc, s0, nseq, seqlen, nb, tblk, zero_init):
    ntb = seqlen // tblk
    kern = functools.partial(_scan_kernel, nb=nb, tblk=tblk, zero_init=zero_init)
    return pl.pallas_call(
        kern,
        grid=(nseq // nb, ntb),
        in_specs=[pl.BlockSpec((None, 6, nb, tblk, RW_WIDTH), lambda g, t: (0, 0, g, t, 0)),
                  pl.BlockSpec((None, 6, nb, tblk, RW_WIDTH), lambda g, t: (1, 0, g, ntb - 1 - t, 0)),
                  pl.BlockSpec((nb, 2, 2, HEAD_DIM, LANES), lambda g, t: (g, 0, 0, 0, 0))],
        out_specs=[pl.BlockSpec((nb, tblk, RW_WIDTH), lambda g, t: (g, t, 0)),
                   pl.BlockSpec((nb, tblk, RW_WIDTH), lambda g, t: (g, ntb - 1 - t, 0)),
                   pl.BlockSpec((nb, 2, 2, HEAD_DIM, LANES), lambda g, t: (g, 0, 0, 0, 0))],
        out_shape=[jax.ShapeDtypeStruct((nseq, seqlen, RW_WIDTH), F32),
                   jax.ShapeDtypeStruct((nseq, seqlen, RW_WIDTH), F32),
                   jax.ShapeDtypeStruct((nseq, 2, 2, HEAD_DIM, LANES), F32)],
        scratch_shapes=[pltpu.VMEM((nb, 2, 2, HEAD_DIM, LANES), F32)],
        compiler_params=_params(("parallel", "arbitrary")),
        name="rwkv_scan",
    )(sc, sc, s0)


def _state_to_pairs(s):
    b = s.shape[0]
    return s.reshape(b, 2, 2, 2, HEAD_DIM, HEAD_DIM).transpose(0, 1, 2, 4, 3, 5).reshape(b, 2, 2, HEAD_DIM, LANES)


def _pairs_to_state(s):
    b = s.shape[0]
    return s.reshape(b, 2, 2, HEAD_DIM, 2, HEAD_DIM).transpose(0, 1, 2, 4, 3, 5).reshape(b, 2, RW_HEADS, HEAD_DIM, HEAD_DIM)


def _rope(x, cos, sin):
    n = x.shape[1]
    m = n // LANES
    if m > 1:
        cos = jnp.concatenate([cos] * m, axis=1)
        sin = jnp.concatenate([sin] * m, axis=1)
    lane = lax.broadcasted_iota(jnp.int32, x.shape, 1)
    swapped = jnp.where((lane % 32) < 16, pltpu.roll(x, n - 16, axis=1), pltpu.roll(x, 16, axis=1))
    return x * cos + swapped * sin


def _attend(q, keys, vals, masks, sink_ref, o_ref):
    scale = HEAD_DIM ** -0.5
    qb = q.astype(BF16)
    kbs = [k.astype(BF16) for k in keys]
    vbs = [v.astype(BF16) for v in vals]
    for h in range(ATT_HEADS):
        hk = h // ATT_GQ
        qh = qb[:, h * HEAD_DIM:(h + 1) * HEAD_DIM]
        ss = []
        for kb, mk in zip(kbs, masks):
            s = lax.dot_general(qh, kb[:, hk * HEAD_DIM:(hk + 1) * HEAD_DIM],
                                (((1,), (1,)), ((), ())), preferred_element_type=F32) * scale
            if mk is not None:
                s = jnp.where(mk, s, NEG_INF)
            ss.append(s)
        sink = sink_ref[h:h + 1, 0:1]
        m = sink
        for s in ss:
            m = jnp.maximum(m, jnp.max(s, axis=-1, keepdims=True))
        den = jnp.exp(sink - m)
        acc = jnp.zeros((q.shape[0], HEAD_DIM), F32)
        for s, vb in zip(ss, vbs):
            p = jnp.exp(s - m)
            den = den + jnp.sum(p, axis=-1, keepdims=True)
            acc = acc + jnp.dot(p.astype(BF16), vb[:, hk * HEAD_DIM:(hk + 1) * HEAD_DIM],
                                preferred_element_type=F32)
        o_ref[:, h * HEAD_DIM:(h + 1) * HEAD_DIM] = acc / den


def _ctx_attn_kernel(q_ref, k_ref, v_ref, sink_ref, o_ref):
    _attend(q_ref[...], [k_ref[...]], [v_ref[...]], [None], sink_ref, o_ref)


def _ctx_attention(q, k, v, sink_b, layer):
    nq = SEQ // ATT_BLOCK
    return pl.pallas_call(
        _ctx_attn_kernel,
        grid=(BATCH, nq),
        in_specs=[pl.BlockSpec((ATT_BLOCK, ATT_WIDTH), lambda b, i: (b * nq + i, 0)),
                  pl.BlockSpec((SEQ, KV_WIDTH), lambda b, i: (b, 0)),
                  pl.BlockSpec((SEQ, KV_WIDTH), lambda b, i: (b, 0)),
                  pl.BlockSpec((None, ATT_HEADS, LANES), lambda b, i: (layer, 0, 0))],
        out_specs=pl.BlockSpec((ATT_BLOCK, ATT_WIDTH), lambda b, i: (b * nq + i, 0)),
        out_shape=jax.ShapeDtypeStruct((N_CTX_ROWS, ATT_WIDTH), F32),
        compiler_params=_params(("parallel", "parallel")),
        name="ctx_attention",
    )(q, k, v, sink_b)


def _lat_attn_kernel(q_ref, k_ref, v_ref, ck_ref, cv_ref, cos_ref, sin_ref, sink_ref, o_ref):
    n = pl.program_id(1)
    nb = pl.num_programs(1)
    r0 = pl.multiple_of(n * ATT_BLOCK, ATT_BLOCK)
    q = _rope(q_ref[...], cos_ref[pl.ds(r0, ATT_BLOCK), :], sin_ref[pl.ds(r0, ATT_BLOCK), :])
    qi = lax.broadcasted_iota(jnp.int32, (ATT_BLOCK, ATT_BLOCK), 0)
    ki = lax.broadcasted_iota(jnp.int32, (ATT_BLOCK, ATT_BLOCK), 1)
    keys, vals, masks = [], [], []
    for j in (-1, 0, 1):
        blk = jnp.clip(n + j, 0, nb - 1)
        k0 = pl.multiple_of(blk * ATT_BLOCK, ATT_BLOCK)
        kb = _rope(k_ref[pl.ds(k0, ATT_BLOCK), :], cos_ref[pl.ds(k0, ATT_BLOCK), :],
                   sin_ref[pl.ds(k0, ATT_BLOCK), :])
        keys.append(kb)
        vals.append(v_ref[pl.ds(k0, ATT_BLOCK), :])
        rel = ki + j * ATT_BLOCK - qi
        reach = jnp.where((n + j >= 0) & (n + j < nb), WINDOW, -1)
        masks.append(jnp.abs(rel) <= reach)
    keys.append(ck_ref[...])
    vals.append(cv_ref[...])
    masks.append(None)
    _attend(q, keys, vals, masks, sink_ref, o_ref)


def _lat_attention(q, k, v, cache_k4, cache_v4, cos, sin, sink_b, layer):
    nb = DEC_SEQ // ATT_BLOCK
    qoff = N_CTX_ROWS // ATT_BLOCK
    soff = N_CTX_ROWS // DEC_SEQ
    return pl.pallas_call(
        _lat_attn_kernel,
        grid=(DEC_BATCH, nb),
        in_specs=[pl.BlockSpec((ATT_BLOCK, ATT_WIDTH), lambda b, i: (qoff + b * nb + i, 0)),
                  pl.BlockSpec((DEC_SEQ, KV_WIDTH), lambda b, i: (soff + b, 0)),
                  pl.BlockSpec((DEC_SEQ, KV_WIDTH), lambda b, i: (soff + b, 0)),
                  pl.BlockSpec((None, None, PAST_LEN, KV_WIDTH), lambda b, i: (b, layer, 0, 0)),
                  pl.BlockSpec((None, None, PAST_LEN, KV_WIDTH), lambda b, i: (b, layer, 0, 0)),
                  pl.BlockSpec((DEC_SEQ, LANES), lambda b, i: (0, 0)),
                  pl.BlockSpec((DEC_SEQ, LANES), lambda b, i: (0, 0)),
                  pl.BlockSpec((None, ATT_HEADS, LANES), lambda b, i: (layer, 0, 0))],
        out_specs=pl.BlockSpec((ATT_BLOCK, ATT_WIDTH), lambda b, i: (b * nb + i, 0)),
        out_shape=jax.ShapeDtypeStruct((N_LAT_ROWS, ATT_WIDTH), F32),
        compiler_params=_params(("parallel", "parallel")),
        name="lat_attention",
    )(q, k, v, cache_k4, cache_v4, cos, sin, sink_b)


def _rope_tables():
    pos = jnp.arange(DEC_SEQ)
    rows = (pos // GRID_W).astype(F32)
    cols = (pos % GRID_W).astype(F32)
    inv = ROPE_BASE ** (-jnp.arange(0, 32, 2, dtype=F32) / 32)
    lane = jnp.arange(LANES)
    posl = jnp.where(((lane % HEAD_DIM) // 32 == 0)[None, :], rows[:, None], cols[:, None])
    ang = posl * inv[lane % 16][None, :]
    sign = jnp.where((lane % 32) < 16, -1.0, 1.0)[None, :]
    return jnp.cos(ang), jnp.sin(ang) * sign


def _outproj_kernel(x_ref, conv_ref, cpe_ref, cne_ref, att_ref, yf_ref, yb_ref, bonus_ref, g_ref,
                    cw_ref, lng_ref, lnb_ref, wout_ref, gate_ref, sh_ref, sc_ref, nf_ref, rtw_ref, rtb_ref,
                    xo_ref, h3_ref, route_ref, counts_ref, cnt_ref):
    conv = conv_ref[...]
    u = conv[:, 256:512] * conv[:, 512:768]
    pe = cpe_ref[0]
    ne = cne_ref[0]
    before, after = _shift_rows(u, pe[:, 256:512] * pe[:, 512:768], ne[:, 256:512] * ne[:, 512:768])
    y_conv = conv[:, 0:256] * (before * cw_ref[0:1, :] + u * cw_ref[1:2, :] + after * cw_ref[2:3, :])

    ones = _head_ones(RW_WIDTH)
    inv_hd = 1.0 / HEAD_DIM
    ysum = yf_ref[...] + yb_ref[...]
    mean = _head_sum(ysum, ones) * inv_hd
    cen = ysum - mean
    var = _head_sum(cen * cen, ones) * inv_hd
    yn = cen * lax.rsqrt(var + GN_EPS) * lng_ref[...] + lnb_ref[...]
    y_rw = (yn + bonus_ref[...]) * jax.nn.sigmoid(g_ref[...])

    mix = (jnp.dot(y_conv.astype(BF16), wout_ref[0:256, :], preferred_element_type=F32)
           + jnp.dot(att_ref[...].astype(BF16), wout_ref[256:768, :], preferred_element_type=F32)
           + jnp.dot(y_rw.astype(BF16), wout_ref[768:1024, :], preferred_element_type=F32))
    x = x_ref[...] + gate_ref[0] * mix
    xo_ref[...] = x
    h2 = _rms(x, nf_ref[...]) * (1.0 + sc_ref[0]) + sh_ref[0]
    for j in range(D_CHUNKS):
        h3_ref[:, j, :] = h2[:, j * LANES:(j + 1) * LANES]

    logits = jnp.dot(h2, rtw_ref[...], preferred_element_type=F32, precision=HIGHEST) + rtb_ref[...]
    lane = lax.broadcasted_iota(jnp.int32, logits.shape, 1)
    big = jnp.int32(1 << 20)
    is_g = lane < N_GROUPS
    gmax = jnp.max(jnp.where(is_g, logits, NEG_INF), axis=-1, keepdims=True)
    gsum = jnp.sum(jnp.where(is_g, jnp.exp(logits - gmax), 0.0), axis=-1, keepdims=True)
    g_top = 1.0 / gsum
    g_idx = jnp.min(jnp.where(is_g & (logits == gmax), lane, big), axis=-1, keepdims=True)
    lo = N_GROUPS + g_idx * EXP_PER_GROUP
    in_grp = (lane >= lo) & (lane < lo + EXP_PER_GROUP)
    emax = jnp.max(jnp.where(in_grp, logits, NEG_INF), axis=-1, keepdims=True)
    pe_ = jnp.where(in_grp, jnp.exp(logits - emax), -1.0)
    p1 = jnp.max(pe_, axis=-1, keepdims=True)
    l1 = jnp.min(jnp.where(pe_ == p1, lane, big), axis=-1, keepdims=True)
    pe2 = jnp.where(lane == l1, -1.0, pe_)
    p2 = jnp.max(pe2, axis=-1, keepdims=True)
    l2 = jnp.min(jnp.where(pe2 == p2, lane, big), axis=-1, keepdims=True)
    wsum = p1 + p2
    w1 = g_top * p1 / wsum
    w2 = g_top * p2 / wsum
    e1 = (l1 - N_GROUPS).astype(F32)
    e2 = (l2 - N_GROUPS).astype(F32)

    @pl.when(pl.program_id(0) == 0)
    def _():
        cnt_ref[...] = jnp.zeros_like(cnt_ref)

    hit1 = lane == l1 - N_GROUPS
    hit2 = lane == l2 - N_GROUPS
    onehot = jnp.where(hit1 | hit2, 1.0, 0.0)
    ri = lax.broadcasted_iota(jnp.int32, (TM, TM), 0)
    ci = lax.broadcasted_iota(jnp.int32, (TM, TM), 1)
    earlier = jnp.where(ci < ri, 1.0, 0.0).astype(BF16)
    before = jnp.dot(earlier, onehot.astype(BF16), preferred_element_type=F32) + cnt_ref[...]
    r1 = jnp.sum(jnp.where(hit1, before, 0.0), axis=-1, keepdims=True)
    r2 = jnp.sum(jnp.where(hit2, before, 0.0), axis=-1, keepdims=True)
    cnt_ref[...] = cnt_ref[...] + jnp.sum(onehot, axis=0, keepdims=True)
    counts_ref[...] = jnp.broadcast_to(cnt_ref[...], counts_ref.shape)

    route = jnp.zeros_like(logits)
    for j, col in enumerate((e1, e2, w1, w2, r1, r2)):
        route = jnp.where(lane == j, col, route)
    route_ref[...] = route


def _outproj(x, conv, cpe, cne, att, yf, yb, bonus, rw, mod3, prm, layer):
    conv_w, ln_g, ln_b, w_out_bf, norm_ffn, rt_w, rt_b = prm
    row = lambda w: pl.BlockSpec((TM, w), lambda i: (i, 0))
    edge = pl.BlockSpec((1, 1, 768), lambda i: (i, 0, 0))
    return pl.pallas_call(
        _outproj_kernel,
        grid=(NT,),
        in_specs=[row(D_MODEL), row(768), edge, edge, row(ATT_WIDTH), row(RW_WIDTH), row(RW_WIDTH),
                  row(RW_WIDTH), pl.BlockSpec((TM, RW_WIDTH), lambda i: (i, 4)),
                  _layer_spec((3, CONV_CH), layer), _layer_spec((1, RW_WIDTH), layer),
                  _layer_spec((1, RW_WIDTH), layer), _layer_spec((D_MODEL, D_MODEL), layer),
                  _mod_spec(layer, 2), _mod_spec(layer, 3), _mod_spec(layer, 4),
                  _layer_spec((1, D_MODEL), layer), _layer_spec((D_MODEL, LANES), layer),
                  _layer_spec((1, LANES), layer)],
        out_specs=[row(D_MODEL), pl.BlockSpec((TM, D_CHUNKS, LANES), lambda i: (i, 0, 0)), row(LANES),
                   pl.BlockSpec((SUBLANES, LANES), lambda i: (0, 0))],
        out_shape=[jax.ShapeDtypeStruct((N_ROWS, D_MODEL), F32),
                   jax.ShapeDtypeStruct((N_ROWS, D_CHUNKS, LANES), F32),
                   jax.ShapeDtypeStruct((N_ROWS, LANES), F32),
                   jax.ShapeDtypeStruct((SUBLANES, LANES), F32)],
        scratch_shapes=[pltpu.VMEM((1, LANES), F32)],
        compiler_params=_params(("arbitrary",)),
        name="outproj_router",
    )(x, conv, cpe, cne, att, yf, yb, bonus, rw, conv_w, ln_g, ln_b, w_out_bf, mod3, mod3, mod3,
      norm_ffn, rt_w, rt_b)


def _slot_plan(route, counts):
    e = route[:, 0:2].astype(jnp.int32)
    rank = route[:, 4:6].astype(jnp.int32)
    counts = counts[0, 0:N_EXPERTS].astype(jnp.int32)
    padded = (counts + MOE_BLOCK - 1) // MOE_BLOCK * MOE_BLOCK
    pad_end = jnp.cumsum(padded)
    pad_start = pad_end - padded
    slots = (pad_start[e] + rank).astype(jnp.int32)
    blk0 = jnp.arange(N_MOE_BLOCKS, dtype=jnp.int32) * MOE_BLOCK
    block_exp = jnp.minimum(jnp.searchsorted(pad_end, blk0, side='right'), N_EXPERTS - 1).astype(jnp.int32)
    n_used = (pad_end[-1] // MOE_BLOCK).astype(jnp.int32)
    return slots.reshape(-1), block_exp, n_used.reshape(1)


def _dispatch_kernel(slots_ref, h3_ref, xs_in_ref, xs_ref, sem):
    del xs_in_ref
    i = pl.program_id(0)
    base = i * (TM * TOP_K)

    def issue(r, c):
        for k in range(TOP_K):
            s = slots_ref[base + r * TOP_K + k]
            pltpu.make_async_copy(h3_ref.at[pl.ds(r, 1)], xs_ref.at[pl.ds(s, 1)], sem).start()
        return c

    lax.fori_loop(0, TM, issue, 0)

    def drain(r, c):
        for k in range(TOP_K):
            pltpu.make_async_copy(h3_ref.at[pl.ds(0, 1)], xs_ref.at[pl.ds(0, 1)], sem).wait()
        return c

    lax.fori_loop(0, TM, drain, 0)


def _dispatch(slots, h3, xs_zero):
    return pl.pallas_call(
        _dispatch_kernel,
        grid_spec=pltpu.PrefetchScalarGridSpec(
            num_scalar_prefetch=1,
            grid=(NT,),
            in_specs=[pl.BlockSpec((TM, D_CHUNKS, LANES), lambda i, s: (i, 0, 0)),
                      pl.BlockSpec(memory_space=pl.ANY)],
            out_specs=pl.BlockSpec(memory_space=pl.ANY),
            scratch_shapes=[pltpu.SemaphoreType.DMA(())]),
        out_shape=jax.ShapeDtypeStruct((N_SLOTS, D_CHUNKS, LANES), F32),
        input_output_aliases={2: 0},
        compiler_params=_params(("arbitrary",)),
        name="moe_dispatch",
    )(slots, h3, xs_zero)


def _expert_kernel(be_ref, nu_ref, xs_ref, w1_ref, w3_ref, w2_ref, ys_ref):
    i = pl.program_id(0)

    @pl.when(i < nu_ref[0])
    def _():
        a = jnp.zeros((MOE_BLOCK, D_EXPERT), F32)
        b = jnp.zeros((MOE_BLOCK, D_EXPERT), F32)
        for j in range(D_CHUNKS):
            xj = xs_ref[:, j, :].astype(BF16)
            a = a + jnp.dot(xj, w1_ref[j * LANES:(j + 1) * LANES, :].astype(BF16), preferred_element_type=F32)
            b = b + jnp.dot(xj, w3_ref[j * LANES:(j + 1) * LANES, :].astype(BF16), preferred_element_type=F32)
        hid = (a * jax.nn.sigmoid(a) * b).astype(BF16)
        w2 = w2_ref[...].astype(BF16)
        for j in range(D_CHUNKS):
            ys_ref[:, j, :] = jnp.dot(hid, w2[:, j * LANES:(j + 1) * LANES], preferred_element_type=F32)

    @pl.when(i >= nu_ref[0])
    def _():
        ys_ref[...] = jnp.zeros_like(ys_ref)


def _experts(block_exp, n_used, xs, w1, w3, w2, layer):
    def wmap(i, be, nu):
        return (layer, be[jnp.minimum(i, nu[0] - 1)], 0, 0)

    return pl.pallas_call(
        _expert_kernel,
        grid_spec=pltpu.PrefetchScalarGridSpec(
            num_scalar_prefetch=2,
            grid=(N_MOE_BLOCKS,),
            in_specs=[pl.BlockSpec((MOE_BLOCK, D_CHUNKS, LANES), lambda i, be, nu: (i, 0, 0)),
                      pl.BlockSpec((None, None, D_MODEL, D_EXPERT), wmap),
                      pl.BlockSpec((None, None, D_MODEL, D_EXPERT), wmap),
                      pl.BlockSpec((None, None, D_EXPERT, D_MODEL), wmap)],
            out_specs=pl.BlockSpec((MOE_BLOCK, D_CHUNKS, LANES), lambda i, be, nu: (i, 0, 0))),
        out_shape=jax.ShapeDtypeStruct((N_SLOTS, D_CHUNKS, LANES), F32),
        compiler_params=_params(("arbitrary",)),
        name="moe_experts",
    )(block_exp, n_used, xs, w1, w3, w2)


def _combine_kernel(slots_ref, x_ref, route_ref, gate_ref, g_ref, ys_ref, o_ref, buf, sem, *, final_norm):
    i = pl.program_id(0)
    base = i * (TM * TOP_K)

    def issue(r, c):
        for k in range(TOP_K):
            s = slots_ref[base + r * TOP_K + k]
            pltpu.make_async_copy(ys_ref.at[pl.ds(s, 1)], buf.at[k, pl.ds(r, 1)], sem).start()
        return c

    lax.fori_loop(0, TM, issue, 0)

    def drain(r, c):
        for k in range(TOP_K):
            pltpu.make_async_copy(ys_ref.at[pl.ds(0, 1)], buf.at[k, pl.ds(0, 1)], sem).wait()
        return c

    lax.fori_loop(0, TM, drain, 0)
    route = route_ref[...]
    w1 = route[:, 2:3]
    w2 = route[:, 3:4]
    y = jnp.concatenate([w1 * buf[0, :, j, :] + w2 * buf[1, :, j, :] for j in range(D_CHUNKS)], axis=1)
    x = x_ref[...] + gate_ref[0] * y
    if final_norm:
        x = _rms(x, g_ref[...])
    o_ref[...] = x


def _combine(slots, x, route, mod3, norm_out, ys, layer, final_norm):
    kern = functools.partial(_combine_kernel, final_norm=final_norm)
    return pl.pallas_call(
        kern,
        grid_spec=pltpu.PrefetchScalarGridSpec(
            num_scalar_prefetch=1,
            grid=(NT,),
            in_specs=[pl.BlockSpec((TM, D_MODEL), lambda i, s: (i, 0)),
                      pl.BlockSpec((TM, LANES), lambda i, s: (i, 0)),
                      pl.BlockSpec((1, 1, D_MODEL),
                                   lambda i, s: ((layer * N_COND + _cond_of_tile(i)) * 6 + 5, 0, 0)),
                      pl.BlockSpec((1, D_MODEL), lambda i, s: (0, 0)),
                      pl.BlockSpec(memory_space=pl.ANY)],
            out_specs=pl.BlockSpec((TM, D_MODEL), lambda i, s: (i, 0)),
            scratch_shapes=[pltpu.VMEM((TOP_K, TM, D_CHUNKS, LANES), F32),
                            pltpu.SemaphoreType.DMA(())]),
        out_shape=jax.ShapeDtypeStruct((N_ROWS, D_MODEL), F32),
        compiler_params=_params(("arbitrary",)),
        name="moe_combine",
    )(slots, x, route, mod3, norm_out.reshape(1, D_MODEL), ys)


def kernel(x_prompt, x_sample, cache_k, cache_v, state_wkv, c, c_ctx, w_mod, b_mod, norm_mix, norm_ffn,
           norm_out, w_in, w_out, conv_w, attn_sink, rw_mu, rw_w0, rw_w1, rw_w2, rw_a0, rw_a1, rw_a2,
           rw_kk, rw_ka, rw_rk, rw_ln_g, rw_ln_b, rt_group_w, rt_group_b, rt_exp_w, rt_exp_b,
           exp_w1, exp_w3, exp_w2):
    cond = jnp.concatenate([c_ctx[None, :], c, jnp.zeros((N_COND - 1 - DEC_BATCH, D_MODEL), F32)], axis=0)
    mod3 = _modulation(cond, w_mod, b_mod).reshape(DEPTH * N_COND * 6, 1, D_MODEL)

    w_in_bf = w_in.astype(BF16)
    w_out_bf = w_out.astype(BF16)
    pad = LANES - N_GROUPS - N_EXPERTS
    rt_w = jnp.concatenate([rt_group_w, rt_exp_w, jnp.zeros((DEPTH, D_MODEL, pad), F32)], axis=-1)
    rt_b = jnp.concatenate([rt_group_b, rt_exp_b, jnp.zeros((DEPTH, pad), F32)], axis=-1).reshape(DEPTH, 1, LANES)
    sink_b = jnp.broadcast_to(attn_sink[:, :, None], (DEPTH, ATT_HEADS, LANES))
    cos, sin = _rope_tables()
    cache_k4 = cache_k.reshape(DEC_BATCH, DEPTH, PAST_LEN, KV_WIDTH)
    cache_v4 = cache_v.reshape(DEC_BATCH, DEPTH, PAST_LEN, KV_WIDTH)
    s0_lat = _state_to_pairs(state_wkv.transpose(1, 0, 2, 3, 4, 5).reshape(DEPTH * DEC_BATCH, 2, RW_HEADS, HEAD_DIM, HEAD_DIM))
    s0_lat = s0_lat.reshape(DEPTH, DEC_BATCH, 2, 2, HEAD_DIM, LANES)
    s0_ctx = jnp.zeros((BATCH, 2, 2, HEAD_DIM, LANES), F32)
    xs_zero = jnp.zeros((N_SLOTS, D_CHUNKS, LANES), F32)
    rw_prm = (rw_mu, rw_w0, rw_w1, rw_w2, rw_a0, rw_a1, rw_a2, rw_kk, rw_ka, rw_rk)
    out_prm = (conv_w, rw_ln_g.reshape(DEPTH, 1, RW_WIDTH), rw_ln_b.reshape(DEPTH, 1, RW_WIDTH), w_out_bf,
               norm_ffn.reshape(DEPTH, 1, D_MODEL), rt_w, rt_b)

    x = jnp.concatenate([x_prompt.reshape(N_CTX_ROWS, D_MODEL), x_sample.reshape(N_LAT_ROWS, D_MODEL)], axis=0)
    keys, vals, states = [], [], []
    for l in range(DEPTH):
        conv, q, k, v, rw = _inproj(x, mod3, norm_mix, w_in_bf, l)
        cpe, cne = _tile_edges(conv)
        rpe, rne = _tile_edges(rw)
        rpe, rne = rpe[:, :, 0:4 * RW_WIDTH], rne[:, :, 0:4 * RW_WIDTH]

        sc_c, bonus_c = _rwkv_pre(rw, rpe, rne, rw_prm, l, 0, NT_CTX)
        sc_l, bonus_l = _rwkv_pre(rw, rpe, rne, rw_prm, l, NT_CTX, NT - NT_CTX)
        yf_c, yb_c, sfin = _rwkv_scan(sc_c.reshape(2, 6, BATCH, SEQ, RW_WIDTH), s0_ctx, BATCH, SEQ, 4, SEQ, True)
        yf_l, yb_l, _ = _rwkv_scan(sc_l.reshape(2, 6, DEC_BATCH, DEC_SEQ, RW_WIDTH), s0_lat[l], DEC_BATCH,
                                   DEC_SEQ, 4, 256, False)
        yf = jnp.concatenate([yf_c.reshape(N_CTX_ROWS, RW_WIDTH), yf_l.reshape(N_LAT_ROWS, RW_WIDTH)], axis=0)
        yb = jnp.concatenate([yb_c.reshape(N_CTX_ROWS, RW_WIDTH), yb_l.reshape(N_LAT_ROWS, RW_WIDTH)], axis=0)
        bonus = jnp.concatenate([bonus_c, bonus_l], axis=0)

        att_c = _ctx_attention(q, k, v, sink_b, l)
        att_l = _lat_attention(q, k, v, cache_k4, cache_v4, cos, sin, sink_b, l)
        att = jnp.concatenate([att_c, att_l], axis=0)

        x_mid, h3, route, counts = _outproj(x, conv, cpe, cne, att, yf, yb, bonus, rw, mod3, out_prm, l)
        slots, block_exp, n_used = _slot_plan(route, counts)
        xs = _dispatch(slots, h3, xs_zero)
        ys = _experts(block_exp, n_used, xs, exp_w1, exp_w3, exp_w2, l)
        x = _combine(slots, x_mid, route, mod3, norm_out, ys, l, l == DEPTH - 1)

        keys.append(k[:N_CTX_ROWS].reshape(BATCH, SEQ, ATT_KV_HEADS, HEAD_DIM))
        vals.append(v[:N_CTX_ROWS].reshape(BATCH, SEQ, ATT_KV_HEADS, HEAD_DIM))
        states.append(_pairs_to_state(sfin))

    y_prompt = x[:N_CTX_ROWS].reshape(BATCH, SEQ, D_MODEL)
    y_sample = x[N_CTX_ROWS:].reshape(DEC_BATCH, DEC_SEQ, D_MODEL)
    return (y_prompt, y_sample, jnp.stack(keys, axis=1), jnp.stack(vals, axis=1), jnp.stack(states, axis=1))
```

```python
import functools

import jax
import jax.numpy as jnp
from jax import lax
from jax.experimental import pallas as pl
from jax.experimental.pallas import tpu as pltpu

F32 = jnp.float32
BF16 = jnp.bfloat16
HIGHEST = lax.Precision.HIGHEST

D_MODEL = 1024
BATCH = 16
SEQ = 256
DEPTH = 4
DEC_BATCH = 4
DEC_SEQ = 2048
PAST_LEN = 256
GRID_W = 64
HEAD_DIM = 64
CONV_CH = 256
ATT_WIDTH = 512
ATT_HEADS = 8
ATT_KV_HEADS = 2
ATT_GQ = 4
KV_WIDTH = 128
WINDOW = 128
ATT_BLOCK = 128
RW_WIDTH = 256
RW_HEADS = 4
LORA = 64
IN_WIDTH = 2816
N_GROUPS = 4
EXP_PER_GROUP = 8
N_EXPERTS = 32
TOP_K = 2
D_EXPERT = 512
ROPE_BASE = 10000.0
NORM_EPS = 1e-6
GN_EPS = 64e-5
DECAY_SCALE = 0.6065306597126334
NEG_INF = -1e30

LANES = 128
SUBLANES = 8
N_CTX_ROWS = BATCH * SEQ
N_LAT_ROWS = DEC_BATCH * DEC_SEQ
N_ROWS = N_CTX_ROWS + N_LAT_ROWS
TM = 256
NT = N_ROWS // TM
NT_CTX = N_CTX_ROWS // TM
TILES_PER_LAT = DEC_SEQ // TM
N_COND = 8
D_CHUNKS = D_MODEL // LANES
MOE_BLOCK = 256
N_ASG = N_ROWS * TOP_K
N_MOE_BLOCKS = N_ASG // MOE_BLOCK + N_EXPERTS
N_SLOTS = N_MOE_BLOCKS * MOE_BLOCK
VMEM_LIMIT = 56 * 1024 * 1024


def _params(sem, vmem=VMEM_LIMIT):
    return pltpu.CompilerParams(dimension_semantics=sem, vmem_limit_bytes=vmem)


def _cond_of_tile(i):
    return jnp.where(i < NT_CTX, 0, 1 + (i - NT_CTX) // TILES_PER_LAT)


def _mod_spec(layer, j):
    return pl.BlockSpec((1, 1, D_MODEL),
                        lambda i: ((layer * N_COND + _cond_of_tile(i)) * 6 + j, 0, 0))


def _layer_spec(shape, layer):
    nd = len(shape)
    return pl.BlockSpec((None,) + tuple(shape), lambda *_: (layer,) + (0,) * nd)


def _rms(x, g):
    return x * lax.rsqrt(jnp.mean(x * x, axis=-1, keepdims=True) + NORM_EPS) * g


def _bdot(a, b):
    return jnp.dot(a.astype(BF16), b.astype(BF16), preferred_element_type=F32)


def _mod_kernel(c_ref, w_ref, b_ref, o_ref):
    c = c_ref[...]
    s = c * jax.nn.sigmoid(c)
    o_ref[...] = jnp.dot(s, w_ref[...], preferred_element_type=F32, precision=HIGHEST) + b_ref[...]


def _modulation(cond, w_mod, b_mod):
    tn = 1536
    return pl.pallas_call(
        _mod_kernel,
        grid=(DEPTH, 6 * D_MODEL // tn),
        in_specs=[pl.BlockSpec((N_COND, D_MODEL), lambda l, j: (0, 0)),
                  pl.BlockSpec((None, D_MODEL, tn), lambda l, j: (l, 0, j)),
                  pl.BlockSpec((None, 1, tn), lambda l, j: (l, 0, j))],
        out_specs=pl.BlockSpec((None, N_COND, tn), lambda l, j: (l, 0, j)),
        out_shape=jax.ShapeDtypeStruct((DEPTH, N_COND, 6 * D_MODEL), F32),
        compiler_params=_params(("arbitrary", "arbitrary")),
        name="modulation",
    )(cond, w_mod, b_mod.reshape(DEPTH, 1, 6 * D_MODEL))


def _inproj_kernel(x_ref, sh_ref, sc_ref, g_ref, w_ref, conv_ref, q_ref, k_ref, v_ref, rw_ref):
    h = _rms(x_ref[...], g_ref[...]) * (1.0 + sc_ref[0]) + sh_ref[0]
    p = jnp.dot(h.astype(BF16), w_ref[...], preferred_element_type=F32)
    conv_ref[...] = p[:, 0:768]
    q_ref[...] = p[:, 768:1280]
    k_ref[...] = p[:, 1280:1408]
    v_ref[...] = p[:, 1408:1536]
    rw_ref[...] = p[:, 1536:2816]


def _inproj(x, mod3, norm_mix, w_in_bf, layer):
    widths = (768, ATT_WIDTH, KV_WIDTH, KV_WIDTH, 5 * RW_WIDTH)
    return pl.pallas_call(
        _inproj_kernel,
        grid=(NT,),
        in_specs=[pl.BlockSpec((TM, D_MODEL), lambda i: (i, 0)),
                  _mod_spec(layer, 0), _mod_spec(layer, 1),
                  _layer_spec((1, D_MODEL), layer),
                  _layer_spec((D_MODEL, IN_WIDTH), layer)],
        out_specs=[pl.BlockSpec((TM, w), lambda i: (i, 0)) for w in widths],
        out_shape=[jax.ShapeDtypeStruct((N_ROWS, w), F32) for w in widths],
        compiler_params=_params(("parallel",)),
        name="inproj",
    )(x, mod3, mod3, norm_mix.reshape(DEPTH, 1, D_MODEL), w_in_bf)


def _tile_edges(a):
    c = a.shape[-1]
    a3 = a.reshape(NT, TM, c)
    first, last = a3[:, 0], a3[:, TM - 1]
    t = jnp.arange(NT)
    lat = t >= NT_CTX
    pos = (t - NT_CTX) % TILES_PER_LAT
    has_prev = (lat & (pos != 0))[:, None]
    has_next = (lat & (pos != TILES_PER_LAT - 1))[:, None]
    prev = jnp.where(has_prev, jnp.roll(last, 1, axis=0), 0.0)
    nxt = jnp.where(has_next, jnp.roll(first, -1, axis=0), 0.0)
    return prev.reshape(NT, 1, c), nxt.reshape(NT, 1, c)


def _shift_rows(x, prev_row, next_row):
    n = x.shape[0]
    row = lax.broadcasted_iota(jnp.int32, x.shape, 0)
    before = jnp.where(row == 0, prev_row, pltpu.roll(x, 1, axis=0))
    after = jnp.where(row == n - 1, next_row, pltpu.roll(x, n - 1, axis=0))
    return before, after


def _head_ones(n):
    r = lax.broadcasted_iota(jnp.int32, (n, n), 0) // HEAD_DIM
    c = lax.broadcasted_iota(jnp.int32, (n, n), 1) // HEAD_DIM
    return (r == c).astype(F32)


def _head_sum(x, ones):
    return jnp.dot(x, ones, preferred_element_type=F32, precision=HIGHEST)


def _rwpre_kernel(x_ref, pe_ref, ne_ref, mu_ref, w0_ref, w1_ref, w2_ref, a0_ref, a1_ref, a2_ref,
                  kk_ref, ka_ref, rk_ref, sc_ref, bonus_ref):
    x = x_ref[...]
    before, after = _shift_rows(x, pe_ref[0], ne_ref[0])
    ones = _head_ones(RW_WIDTH)
    bonus = jnp.zeros((TM, RW_WIDTH), F32)
    for d in range(2):
        nbr = before if d == 0 else after
        xd = x + (nbr - x) * mu_ref[d:d + 1, :]
        r = xd[:, 0:256]
        k = xd[:, 256:512]
        v = xd[:, 512:768]
        z = xd[:, 768:1024]
        lw = _bdot(jnp.tanh(_bdot(z, w1_ref[d])), w2_ref[d])
        w = jnp.exp(-DECAY_SCALE * jax.nn.sigmoid(w0_ref[d:d + 1, :] + lw))
        la = _bdot(_bdot(z, a1_ref[d]), a2_ref[d])
        a = jax.nn.sigmoid(a0_ref[d:d + 1, :] + la)
        kk = k * kk_ref[d:d + 1, :]
        kk = kk / jnp.maximum(jnp.sqrt(_head_sum(kk * kk, ones)), 1e-12)
        k2 = k * (1.0 + (a - 1.0) * ka_ref[d:d + 1, :])
        bonus = bonus + _head_sum(r * k2 * rk_ref[d:d + 1, :], ones) * v
        sc_ref[d, 0] = r
        sc_ref[d, 1] = w
        sc_ref[d, 2] = k2
        sc_ref[d, 3] = v
        sc_ref[d, 4] = -kk
        sc_ref[d, 5] = kk * a
    bonus_ref[...] = bonus


def _rwkv_pre(rw, prev_e, next_e, prm, layer, tile0, ntiles):
    (mu, w0, w1, w2, a0, a1, a2, kk, ka, rk) = prm
    nrows = ntiles * TM
    return pl.pallas_call(
        _rwpre_kernel,
        grid=(ntiles,),
        in_specs=[pl.BlockSpec((TM, 4 * RW_WIDTH), lambda i: (i + tile0, 0)),
                  pl.BlockSpec((1, 1, 4 * RW_WIDTH), lambda i: (i + tile0, 0, 0)),
                  pl.BlockSpec((1, 1, 4 * RW_WIDTH), lambda i: (i + tile0, 0, 0)),
                  _layer_spec((2, 4 * RW_WIDTH), layer),
                  _layer_spec((2, RW_WIDTH), layer),
                  _layer_spec((2, RW_WIDTH, LORA), layer),
                  _layer_spec((2, LORA, RW_WIDTH), layer),
                  _layer_spec((2, RW_WIDTH), layer),
                  _layer_spec((2, RW_WIDTH, LORA), layer),
                  _layer_spec((2, LORA, RW_WIDTH), layer),
                  _layer_spec((2, RW_WIDTH), layer),
                  _layer_spec((2, RW_WIDTH), layer),
                  _layer_spec((2, RW_WIDTH), layer)],
        out_specs=[pl.BlockSpec((2, 6, TM, RW_WIDTH), lambda i: (0, 0, i, 0)),
                   pl.BlockSpec((TM, RW_WIDTH), lambda i: (i, 0))],
        out_shape=[jax.ShapeDtypeStruct((2, 6, nrows, RW_WIDTH), F32),
                   jax.ShapeDtypeStruct((nrows, RW_WIDTH), F32)],
        compiler_params=_params(("parallel",)),
        name="rwkv_pre",
    )(rw, prev_e, next_e, mu, w0, w1, w2, a0, a1, a2, kk, ka, rk)


def _half_ones():
    r = lax.broadcasted_iota(jnp.int32, (LANES, LANES), 0) // HEAD_DIM
    c = lax.broadcasted_iota(jnp.int32, (LANES, LANES), 1) // HEAD_DIM
    return (r == c).astype(BF16)


def _scan_kernel(scf_ref, scb_ref, s0_ref, yf_ref, yb_ref, sfin_ref, st_ref, *, nb, tblk, zero_init):
    tb = pl.program_id(1)
    ntb = pl.num_programs(1)

    @pl.when(tb == 0)
    def _():
        if zero_init:
            st_ref[...] = jnp.zeros_like(st_ref)
        else:
            st_ref[...] = s0_ref[...]

    ones = _half_ones()
    vi = lax.broadcasted_iota(jnp.int32, (HEAD_DIM, LANES), 0)
    li = lax.broadcasted_iota(jnp.int32, (HEAD_DIM, LANES), 1)
    diag = vi == (li % HEAD_DIM)
    diag_bf = diag.astype(BF16)
    chains = [(b, d, p) for b in range(nb) for d in range(2) for p in range(2)]
    r_row, w_row, k_row, v_row, a_row, b_row = range(6)

    def row_of(t, d):
        return t if d == 0 else tblk - 1 - t

    def load_rows(t, which):
        rows = {}
        for b in range(nb):
            for d in range(2):
                ref = scf_ref if d == 0 else scb_ref
                full = {j: ref[j, b, pl.ds(row_of(t, d), 1), :] for j in which}
                for p in range(2):
                    rows[(b, d, p)] = {j: x[:, p * LANES:(p + 1) * LANES] for j, x in full.items()}
        return rows

    def states():
        return {c: st_ref[c[0], c[1], c[2]] for c in chains}

    def emit_y(t, ss, rows):
        pr = jnp.concatenate([(ss[c] * rows[c][r_row]).astype(BF16) for c in chains], axis=0)
        ybc = jnp.dot(pr, ones, preferred_element_type=F32)
        ys = {}
        for i, c in enumerate(chains):
            blk = ybc[i * HEAD_DIM:(i + 1) * HEAD_DIM]
            ys[c] = jnp.sum(jnp.where(diag, blk, 0.0), axis=0, keepdims=True)
        for b in range(nb):
            for d in range(2):
                y_ref = yf_ref if d == 0 else yb_ref
                y_ref[b, pl.ds(row_of(t, d), 1), :] = jnp.concatenate([ys[(b, d, 0)], ys[(b, d, 1)]], axis=1)

    def step(t, carry):
        rows = load_rows(t, (w_row, k_row, v_row, a_row, b_row))
        tp = jnp.maximum(t - 1, 0)
        ss = states()
        emit_y(tp, ss, load_rows(tp, (r_row,)))
        pa = jnp.concatenate([(ss[c] * rows[c][a_row]).astype(BF16) for c in chains], axis=0)
        dv = jnp.concatenate([diag_bf * rows[c][v_row].astype(BF16) for c in chains], axis=0)
        sa = jnp.dot(pa, ones, preferred_element_type=F32)
        vc = jnp.dot(dv, ones, preferred_element_type=F32)
        for i, c in enumerate(chains):
            rw = rows[c]
            sl = slice(i * HEAD_DIM, (i + 1) * HEAD_DIM)
            st_ref[c[0], c[1], c[2]] = ss[c] * rw[w_row] + sa[sl] * rw[b_row] + vc[sl] * rw[k_row]
        return carry

    lax.fori_loop(0, tblk, step, 0)
    emit_y(tblk - 1, states(), load_rows(tblk - 1, (r_row,)))

    @pl.when(tb == ntb - 1)
    def _():
        sfin_ref[...] = st_ref[...]


def _rwkv_scan(sc, s0, nseq, seqlen, nb, tblk, zero_init):
    ntb = seqlen // tblk
    kern = functools.partial(_scan_kernel, nb=nb, tblk=tblk, zero_init=zero_init)
    return pl.pallas_call(
        kern,
        grid=(nseq // nb, ntb),
        in_specs=[pl.BlockSpec((None, 6, nb, tblk, RW_WIDTH), lambda g, t: (0, 0, g, t, 0)),
                  pl.BlockSpec((None, 6, nb, tblk, RW_WIDTH), lambda g, t: (1, 0, g, ntb - 1 - t, 0)),
                  pl.BlockSpec((nb, 2, 2, HEAD_DIM, LANES), lambda g, t: (g, 0, 0, 0, 0))],
        out_specs=[pl.BlockSpec((nb, tblk, RW_WIDTH), lambda g, t: (g, t, 0)),
                   pl.BlockSpec((nb, tblk, RW_WIDTH), lambda g, t: (g, ntb - 1 - t, 0)),
                   pl.BlockSpec((nb, 2, 2, HEAD_DIM, LANES), lambda g, t: (g, 0, 0, 0, 0))],
        out_shape=[jax.ShapeDtypeStruct((nseq, seqlen, RW_WIDTH), F32),
                   jax.ShapeDtypeStruct((nseq, seqlen, RW_WIDTH), F32),
                   jax.ShapeDtypeStruct((nseq, 2, 2, HEAD_DIM, LANES), F32)],
        scratch_shapes=[pltpu.VMEM((nb, 2, 2, HEAD_DIM, LANES), F32)],
        compiler_params=_params(("parallel", "arbitrary")),
        name="rwkv_scan",
    )(sc, sc, s0)


def _state_to_pairs(s):
    b = s.shape[0]
    return s.reshape(b, 2, 2, 2, HEAD_DIM, HEAD_DIM).transpose(0, 1, 2, 4, 3, 5).reshape(b, 2, 2, HEAD_DIM, LANES)


def _pairs_to_state(s):
    b = s.shape[0]
    return s.reshape(b, 2, 2, HEAD_DIM, 2, HEAD_DIM).transpose(0, 1, 2, 4, 3, 5).reshape(b, 2, RW_HEADS, HEAD_DIM, HEAD_DIM)


def _rope(x, cos, sin):
    n = x.shape[1]
    m = n // LANES
    if m > 1:
        cos = jnp.concatenate([cos] * m, axis=1)
        sin = jnp.concatenate([sin] * m, axis=1)
    lane = lax.broadcasted_iota(jnp.int32, x.shape, 1)
    swapped = jnp.where((lane % 32) < 16, pltpu.roll(x, n - 16, axis=1), pltpu.roll(x, 16, axis=1))
    return x * cos + swapped * sin


def _attend(q, keys, vals, biases, sink_ref, o_ref):
    nq = q.shape[0]
    qb = (q * (HEAD_DIM ** -0.5)).astype(BF16)
    kbs = [k.astype(BF16) for k in keys]
    vbs = [v.astype(BF16) for v in vals]
    tiled = [None if b is None else jnp.concatenate([b] * ATT_GQ, axis=0) for b in biases]
    for hk in range(ATT_KV_HEADS):
        heads = range(hk * ATT_GQ, (hk + 1) * ATT_GQ)
        kv = slice(hk * HEAD_DIM, (hk + 1) * HEAD_DIM)
        qg = jnp.concatenate([qb[:, h * HEAD_DIM:(h + 1) * HEAD_DIM] for h in heads], axis=0)
        sink = jnp.concatenate([jnp.broadcast_to(sink_ref[h:h + 1, 0:1], (nq, 1)) for h in heads], axis=0)
        ss = []
        for kb, bias in zip(kbs, tiled):
            s = lax.dot_general(qg, kb[:, kv], (((1,), (1,)), ((), ())), preferred_element_type=F32)
            ss.append(s if bias is None else s + bias)
        m_lanes = ss[0]
        for s in ss[1:]:
            m_lanes = jnp.maximum(m_lanes, s)
        m = jnp.maximum(jnp.max(m_lanes, axis=-1, keepdims=True), sink)
        l_lanes = None
        acc = jnp.zeros((ATT_GQ * nq, HEAD_DIM), F32)
        for s, vb in zip(ss, vbs):
            p = jnp.exp(s - m)
            l_lanes = p if l_lanes is None else l_lanes + p
            acc = acc + jnp.dot(p.astype(BF16), vb[:, kv], preferred_element_type=F32)
        out = acc / (jnp.sum(l_lanes, axis=-1, keepdims=True) + jnp.exp(sink - m))
        for g, h in enumerate(heads):
            o_ref[:, h * HEAD_DIM:(h + 1) * HEAD_DIM] = out[g * nq:(g + 1) * nq]


def _ctx_attn_kernel(q_ref, k_ref, v_ref, sink_ref, o_ref):
    chunks = range(0, SEQ, ATT_BLOCK)
    _attend(q_ref[...], [k_ref[c:c + ATT_BLOCK, :] for c in chunks], [v_ref[c:c + ATT_BLOCK, :] for c in chunks],
            [None for _ in chunks], sink_ref, o_ref)


def _ctx_attention(q, k, v, sink_b, layer):
    nq = SEQ // ATT_BLOCK
    return pl.pallas_call(
        _ctx_attn_kernel,
        grid=(BATCH, nq),
        in_specs=[pl.BlockSpec((ATT_BLOCK, ATT_WIDTH), lambda b, i: (b * nq + i, 0)),
                  pl.BlockSpec((SEQ, KV_WIDTH), lambda b, i: (b, 0)),
                  pl.BlockSpec((SEQ, KV_WIDTH), lambda b, i: (b, 0)),
                  pl.BlockSpec((None, ATT_HEADS, LANES), lambda b, i: (layer, 0, 0))],
        out_specs=pl.BlockSpec((ATT_BLOCK, ATT_WIDTH), lambda b, i: (b * nq + i, 0)),
        out_shape=jax.ShapeDtypeStruct((N_CTX_ROWS, ATT_WIDTH), F32),
        compiler_params=_params(("parallel", "parallel")),
        name="ctx_attention",
    )(q, k, v, sink_b)


def _lat_attn_kernel(q_ref, k_ref, v_ref, ck_ref, cv_ref, cos_ref, sin_ref, sink_ref, o_ref):
    n = pl.program_id(1)
    nb = pl.num_programs(1)
    r0 = pl.multiple_of(n * ATT_BLOCK, ATT_BLOCK)
    q = _rope(q_ref[...], cos_ref[pl.ds(r0, ATT_BLOCK), :], sin_ref[pl.ds(r0, ATT_BLOCK), :])
    qi = lax.broadcasted_iota(jnp.int32, (ATT_BLOCK, ATT_BLOCK), 0)
    ki = lax.broadcasted_iota(jnp.int32, (ATT_BLOCK, ATT_BLOCK), 1)
    keys, vals, masks = [], [], []
    for j in (-1, 0, 1):
        blk = jnp.clip(n + j, 0, nb - 1)
        k0 = pl.multiple_of(blk * ATT_BLOCK, ATT_BLOCK)
        kb = _rope(k_ref[pl.ds(k0, ATT_BLOCK), :], cos_ref[pl.ds(k0, ATT_BLOCK), :],
                   sin_ref[pl.ds(k0, ATT_BLOCK), :])
        keys.append(kb)
        vals.append(v_ref[pl.ds(k0, ATT_BLOCK), :])
        rel = ki + j * ATT_BLOCK - qi
        reach = jnp.where((n + j >= 0) & (n + j < nb), WINDOW, -1)
        masks.append(jnp.where(jnp.abs(rel) <= reach, 0.0, NEG_INF))
    for c in range(0, PAST_LEN, ATT_BLOCK):
        keys.append(ck_ref[c:c + ATT_BLOCK, :])
        vals.append(cv_ref[c:c + ATT_BLOCK, :])
        masks.append(None)
    _attend(q, keys, vals, masks, sink_ref, o_ref)


def _lat_attention(q, k, v, cache_k4, cache_v4, cos, sin, sink_b, layer):
    nb = DEC_SEQ // ATT_BLOCK
    qoff = N_CTX_ROWS // ATT_BLOCK
    soff = N_CTX_ROWS // DEC_SEQ
    return pl.pallas_call(
        _lat_attn_kernel,
        grid=(DEC_BATCH, nb),
        in_specs=[pl.BlockSpec((ATT_BLOCK, ATT_WIDTH), lambda b, i: (qoff + b * nb + i, 0)),
                  pl.BlockSpec((DEC_SEQ, KV_WIDTH), lambda b, i: (soff + b, 0)),
                  pl.BlockSpec((DEC_SEQ, KV_WIDTH), lambda b, i: (soff + b, 0)),
                  pl.BlockSpec((None, None, PAST_LEN, KV_WIDTH), lambda b, i: (b, layer, 0, 0)),
                  pl.BlockSpec((None, None, PAST_LEN, KV_WIDTH), lambda b, i: (b, layer, 0, 0)),
                  pl.BlockSpec((DEC_SEQ, LANES), lambda b, i: (0, 0)),
                  pl.BlockSpec((DEC_SEQ, LANES), lambda b, i: (0, 0)),
                  pl.BlockSpec((None, ATT_HEADS, LANES), lambda b, i: (layer, 0, 0))],
        out_specs=pl.BlockSpec((ATT_BLOCK, ATT_WIDTH), lambda b, i: (b * nb + i, 0)),
        out_shape=jax.ShapeDtypeStruct((N_LAT_ROWS, ATT_WIDTH), F32),
        compiler_params=_params(("parallel", "parallel")),
        name="lat_attention",
    )(q, k, v, cache_k4, cache_v4, cos, sin, sink_b)


def _rope_tables():
    pos = jnp.arange(DEC_SEQ)
    rows = (pos // GRID_W).astype(F32)
    cols = (pos % GRID_W).astype(F32)
    inv = ROPE_BASE ** (-jnp.arange(0, 32, 2, dtype=F32) / 32)
    lane = jnp.arange(LANES)
    posl = jnp.where(((lane % HEAD_DIM) // 32 == 0)[None, :], rows[:, None], cols[:, None])
    ang = posl * inv[lane % 16][None, :]
    sign = jnp.where((lane % 32) < 16, -1.0, 1.0)[None, :]
    return jnp.cos(ang), jnp.sin(ang) * sign


def _outproj_kernel(x_ref, conv_ref, cpe_ref, cne_ref, att_ref, yf_ref, yb_ref, bonus_ref, g_ref,
                    cw_ref, lng_ref, lnb_ref, wout_ref, gate_ref, sh_ref, sc_ref, nf_ref, rtw_ref, rtb_ref,
                    xo_ref, h3_ref, route_ref, counts_ref, cnt_ref):
    conv = conv_ref[...]
    u = conv[:, 256:512] * conv[:, 512:768]
    pe = cpe_ref[0]
    ne = cne_ref[0]
    before, after = _shift_rows(u, pe[:, 256:512] * pe[:, 512:768], ne[:, 256:512] * ne[:, 512:768])
    y_conv = conv[:, 0:256] * (before * cw_ref[0:1, :] + u * cw_ref[1:2, :] + after * cw_ref[2:3, :])

    ones = _head_ones(RW_WIDTH)
    inv_hd = 1.0 / HEAD_DIM
    ysum = yf_ref[...] + yb_ref[...]
    mean = _head_sum(ysum, ones) * inv_hd
    cen = ysum - mean
    var = _head_sum(cen * cen, ones) * inv_hd
    yn = cen * lax.rsqrt(var + GN_EPS) * lng_ref[...] + lnb_ref[...]
    y_rw = (yn + bonus_ref[...]) * jax.nn.sigmoid(g_ref[...])

    mix = (jnp.dot(y_conv.astype(BF16), wout_ref[0:256, :], preferred_element_type=F32)
           + jnp.dot(att_ref[...].astype(BF16), wout_ref[256:768, :], preferred_element_type=F32)
           + jnp.dot(y_rw.astype(BF16), wout_ref[768:1024, :], preferred_element_type=F32))
    x = x_ref[...] + gate_ref[0] * mix
    xo_ref[...] = x
    h2 = _rms(x, nf_ref[...]) * (1.0 + sc_ref[0]) + sh_ref[0]
    for j in range(D_CHUNKS):
        h3_ref[pl.ds(j, TM, stride=D_CHUNKS), :] = h2[:, j * LANES:(j + 1) * LANES]

    logits = jnp.dot(h2, rtw_ref[...], preferred_element_type=F32, precision=HIGHEST) + rtb_ref[...]
    lane = lax.broadcasted_iota(jnp.int32, logits.shape, 1)
    big = jnp.int32(1 << 20)
    is_g = lane < N_GROUPS
    gmax = jnp.max(jnp.where(is_g, logits, NEG_INF), axis=-1, keepdims=True)
    gsum = jnp.sum(jnp.where(is_g, jnp.exp(logits - gmax), 0.0), axis=-1, keepdims=True)
    g_top = 1.0 / gsum
    g_idx = jnp.min(jnp.where(is_g & (logits == gmax), lane, big), axis=-1, keepdims=True)
    lo = N_GROUPS + g_idx * EXP_PER_GROUP
    in_grp = (lane >= lo) & (lane < lo + EXP_PER_GROUP)
    emax = jnp.max(jnp.where(in_grp, logits, NEG_INF), axis=-1, keepdims=True)
    pe_ = jnp.where(in_grp, jnp.exp(logits - emax), -1.0)
    p1 = jnp.max(pe_, axis=-1, keepdims=True)
    l1 = jnp.min(jnp.where(pe_ == p1, lane, big), axis=-1, keepdims=True)
    pe2 = jnp.where(lane == l1, -1.0, pe_)
    p2 = jnp.max(pe2, axis=-1, keepdims=True)
    l2 = jnp.min(jnp.where(pe2 == p2, lane, big), axis=-1, keepdims=True)
    wsum = p1 + p2
    w1 = g_top * p1 / wsum
    w2 = g_top * p2 / wsum
    e1 = (l1 - N_GROUPS).astype(F32)
    e2 = (l2 - N_GROUPS).astype(F32)

    @pl.when(pl.program_id(0) == 0)
    def _():
        cnt_ref[...] = jnp.zeros_like(cnt_ref)

    hit1 = lane == l1 - N_GROUPS
    hit2 = lane == l2 - N_GROUPS
    onehot = jnp.where(hit1 | hit2, 1.0, 0.0)
    ri = lax.broadcasted_iota(jnp.int32, (TM, TM), 0)
    ci = lax.broadcasted_iota(jnp.int32, (TM, TM), 1)
    earlier = jnp.where(ci < ri, 1.0, 0.0).astype(BF16)
    before = jnp.dot(earlier, onehot.astype(BF16), preferred_element_type=F32) + cnt_ref[...]
    r1 = jnp.sum(jnp.where(hit1, before, 0.0), axis=-1, keepdims=True)
    r2 = jnp.sum(jnp.where(hit2, before, 0.0), axis=-1, keepdims=True)
    cnt_ref[...] = cnt_ref[...] + jnp.sum(onehot, axis=0, keepdims=True)
    counts_ref[...] = jnp.broadcast_to(cnt_ref[...], counts_ref.shape)

    route = jnp.zeros_like(logits)
    for j, col in enumerate((e1, e2, w1, w2, r1, r2)):
        route = jnp.where(lane == j, col, route)
    route_ref[...] = route


def _outproj(x, conv, cpe, cne, att, yf, yb, bonus, rw, mod3, prm, layer):
    conv_w, ln_g, ln_b, w_out_bf, norm_ffn, rt_w, rt_b = prm
    row = lambda w: pl.BlockSpec((TM, w), lambda i: (i, 0))
    edge = pl.BlockSpec((1, 1, 768), lambda i: (i, 0, 0))
    return pl.pallas_call(
        _outproj_kernel,
        grid=(NT,),
        in_specs=[row(D_MODEL), row(768), edge, edge, row(ATT_WIDTH), row(RW_WIDTH), row(RW_WIDTH),
                  row(RW_WIDTH), pl.BlockSpec((TM, RW_WIDTH), lambda i: (i, 4)),
                  _layer_spec((3, CONV_CH), layer), _layer_spec((1, RW_WIDTH), layer),
                  _layer_spec((1, RW_WIDTH), layer), _layer_spec((D_MODEL, D_MODEL), layer),
                  _mod_spec(layer, 2), _mod_spec(layer, 3), _mod_spec(layer, 4),
                  _layer_spec((1, D_MODEL), layer), _layer_spec((D_MODEL, LANES), layer),
                  _layer_spec((1, LANES), layer)],
        out_specs=[row(D_MODEL), pl.BlockSpec((TM * D_CHUNKS, LANES), lambda i: (i, 0)), row(LANES),
                   pl.BlockSpec((SUBLANES, LANES), lambda i: (0, 0))],
        out_shape=[jax.ShapeDtypeStruct((N_ROWS, D_MODEL), F32),
                   jax.ShapeDtypeStruct((N_ROWS * D_CHUNKS, LANES), F32),
                   jax.ShapeDtypeStruct((N_ROWS, LANES), F32),
                   jax.ShapeDtypeStruct((SUBLANES, LANES), F32)],
        scratch_shapes=[pltpu.VMEM((1, LANES), F32)],
        compiler_params=_params(("arbitrary",)),
        name="outproj_router",
    )(x, conv, cpe, cne, att, yf, yb, bonus, rw, conv_w, ln_g, ln_b, w_out_bf, mod3, mod3, mod3,
      norm_ffn, rt_w, rt_b)


def _slot_plan(route, counts):
    e = route[:, 0:2].astype(jnp.int32)
    rank = route[:, 4:6].astype(jnp.int32)
    counts = counts[0, 0:N_EXPERTS].astype(jnp.int32)
    padded = (counts + MOE_BLOCK - 1) // MOE_BLOCK * MOE_BLOCK
    pad_end = jnp.cumsum(padded)
    pad_start = pad_end - padded
    slots = (pad_start[e] + rank).astype(jnp.int32)
    blk0 = jnp.arange(N_MOE_BLOCKS, dtype=jnp.int32) * MOE_BLOCK
    block_exp = jnp.sum((pad_end[None, :] <= blk0[:, None]).astype(jnp.int32), axis=1)
    block_exp = jnp.minimum(block_exp, N_EXPERTS - 1)
    n_used = (pad_end[-1] // MOE_BLOCK).astype(jnp.int32)
    return slots.reshape(-1), block_exp, n_used.reshape(1)


def _dispatch_kernel(slots_ref, h3_ref, xs_in_ref, xs_ref, sem):
    del xs_in_ref
    i = pl.program_id(0)
    base = i * (TM * TOP_K)

    def row_copy(r, s):
        src = h3_ref.at[pl.ds(pl.multiple_of(r * D_CHUNKS, D_CHUNKS), D_CHUNKS)]
        dst = xs_ref.at[pl.ds(pl.multiple_of(s * D_CHUNKS, D_CHUNKS), D_CHUNKS)]
        return pltpu.make_async_copy(src, dst, sem)

    def issue(r, c):
        for k in range(TOP_K):
            row_copy(r, slots_ref[base + r * TOP_K + k]).start(priority=k)
        return c

    lax.fori_loop(0, TM, issue, 0)

    def drain(r, c):
        for k in range(TOP_K):
            row_copy(0, 0).wait()
        return c

    lax.fori_loop(0, TM, drain, 0)


def _dispatch(slots, h3, xs_zero):
    return pl.pallas_call(
        _dispatch_kernel,
        grid_spec=pltpu.PrefetchScalarGridSpec(
            num_scalar_prefetch=1,
            grid=(NT,),
            in_specs=[pl.BlockSpec((TM * D_CHUNKS, LANES), lambda i, s: (i, 0)),
                      pl.BlockSpec(memory_space=pl.ANY)],
            out_specs=pl.BlockSpec(memory_space=pl.ANY),
            scratch_shapes=[pltpu.SemaphoreType.DMA(())]),
        out_shape=jax.ShapeDtypeStruct((N_SLOTS * D_CHUNKS, LANES), F32),
        input_output_aliases={2: 0},
        compiler_params=_params(("arbitrary",)),
        name="moe_dispatch",
    )(slots, h3, xs_zero)


def _expert_kernel(be_ref, nu_ref, xs_ref, w1_ref, w3_ref, w2_ref, ys_ref, w1b, w3b, w2b):
    i = pl.program_id(0)
    n_used = nu_ref[0]
    blk = jnp.minimum(i, n_used - 1)
    expert = be_ref[blk]
    prev_expert = be_ref[jnp.maximum(blk - 1, 0)]

    @pl.when((i < n_used) & ((i == 0) | (expert != prev_expert)))
    def _():
        w1b[...] = w1_ref[...].astype(BF16)
        w3b[...] = w3_ref[...].astype(BF16)
        w2b[...] = w2_ref[...].astype(BF16)

    @pl.when(i < n_used)
    def _():
        x = jnp.concatenate([xs_ref[pl.ds(j, MOE_BLOCK, stride=D_CHUNKS), :].astype(BF16)
                             for j in range(D_CHUNKS)], axis=1)
        a = jnp.dot(x, w1b[...], preferred_element_type=F32)
        b = jnp.dot(x, w3b[...], preferred_element_type=F32)
        hid = (a * jax.nn.sigmoid(a) * b).astype(BF16)
        y = jnp.dot(hid, w2b[...], preferred_element_type=F32)
        for j in range(D_CHUNKS):
            ys_ref[pl.ds(j, MOE_BLOCK, stride=D_CHUNKS), :] = y[:, j * LANES:(j + 1) * LANES]

    @pl.when(i >= n_used)
    def _():
        ys_ref[...] = jnp.zeros_like(ys_ref)


def _experts(block_exp, n_used, xs, w1, w3, w2, layer):
    def wmap(i, be, nu):
        return (layer, be[jnp.minimum(i, nu[0] - 1)], 0, 0)

    return pl.pallas_call(
        _expert_kernel,
        grid_spec=pltpu.PrefetchScalarGridSpec(
            num_scalar_prefetch=2,
            grid=(N_MOE_BLOCKS,),
            in_specs=[pl.BlockSpec((MOE_BLOCK * D_CHUNKS, LANES), lambda i, be, nu: (i, 0)),
                      pl.BlockSpec((None, None, D_MODEL, D_EXPERT), wmap),
                      pl.BlockSpec((None, None, D_MODEL, D_EXPERT), wmap),
                      pl.BlockSpec((None, None, D_EXPERT, D_MODEL), wmap)],
            out_specs=pl.BlockSpec((MOE_BLOCK * D_CHUNKS, LANES), lambda i, be, nu: (i, 0)),
            scratch_shapes=[pltpu.VMEM((D_MODEL, D_EXPERT), BF16), pltpu.VMEM((D_MODEL, D_EXPERT), BF16),
                            pltpu.VMEM((D_EXPERT, D_MODEL), BF16)]),
        out_shape=jax.ShapeDtypeStruct((N_SLOTS * D_CHUNKS, LANES), F32),
        compiler_params=_params(("arbitrary",)),
        name="moe_experts",
    )(block_exp, n_used, xs, w1, w3, w2)


def _combine_kernel(slots_ref, x_ref, route_ref, gate_ref, g_ref, ys_ref, o_ref, buf, sem, *, final_norm):
    i = pl.program_id(0)
    base = i * (TM * TOP_K)

    def row_copy(s, k, r):
        src = ys_ref.at[pl.ds(pl.multiple_of(s * D_CHUNKS, D_CHUNKS), D_CHUNKS)]
        dst = buf.at[k, pl.ds(pl.multiple_of(r * D_CHUNKS, D_CHUNKS), D_CHUNKS)]
        return pltpu.make_async_copy(src, dst, sem)

    def issue(r, c):
        for k in range(TOP_K):
            row_copy(slots_ref[base + r * TOP_K + k], k, r).start(priority=k)
        return c

    lax.fori_loop(0, TM, issue, 0)

    def drain(r, c):
        for k in range(TOP_K):
            row_copy(0, k, 0).wait()
        return c

    lax.fori_loop(0, TM, drain, 0)
    route = route_ref[...]
    w1 = route[:, 2:3]
    w2 = route[:, 3:4]
    y = jnp.concatenate([w1 * buf[0, pl.ds(j, TM, stride=D_CHUNKS), :] + w2 * buf[1, pl.ds(j, TM, stride=D_CHUNKS), :]
                         for j in range(D_CHUNKS)], axis=1)
    x = x_ref[...] + gate_ref[0] * y
    if final_norm:
        x = _rms(x, g_ref[...])
    o_ref[...] = x


def _combine(slots, x, route, mod3, norm_out, ys, layer, final_norm):
    kern = functools.partial(_combine_kernel, final_norm=final_norm)
    return pl.pallas_call(
        kern,
        grid_spec=pltpu.PrefetchScalarGridSpec(
            num_scalar_prefetch=1,
            grid=(NT,),
            in_specs=[pl.BlockSpec((TM, D_MODEL), lambda i, s: (i, 0)),
                      pl.BlockSpec((TM, LANES), lambda i, s: (i, 0)),
                      pl.BlockSpec((1, 1, D_MODEL),
                                   lambda i, s: ((layer * N_COND + _cond_of_tile(i)) * 6 + 5, 0, 0)),
                      pl.BlockSpec((1, D_MODEL), lambda i, s: (0, 0)),
                      pl.BlockSpec(memory_space=pl.ANY)],
            out_specs=pl.BlockSpec((TM, D_MODEL), lambda i, s: (i, 0)),
            scratch_shapes=[pltpu.VMEM((TOP_K, TM * D_CHUNKS, LANES), F32),
                            pltpu.SemaphoreType.DMA(())]),
        out_shape=jax.ShapeDtypeStruct((N_ROWS, D_MODEL), F32),
        compiler_params=_params(("arbitrary",)),
        name="moe_combine",
    )(slots, x, route, mod3, norm_out.reshape(1, D_MODEL), ys)


def kernel(x_prompt, x_sample, cache_k, cache_v, state_wkv, c, c_ctx, w_mod, b_mod, norm_mix, norm_ffn,
           norm_out, w_in, w_out, conv_w, attn_sink, rw_mu, rw_w0, rw_w1, rw_w2, rw_a0, rw_a1, rw_a2,
           rw_kk, rw_ka, rw_rk, rw_ln_g, rw_ln_b, rt_group_w, rt_group_b, rt_exp_w, rt_exp_b,
           exp_w1, exp_w3, exp_w2):
    cond = jnp.concatenate([c_ctx[None, :], c, jnp.zeros((N_COND - 1 - DEC_BATCH, D_MODEL), F32)], axis=0)
    mod3 = _modulation(cond, w_mod, b_mod).reshape(DEPTH * N_COND * 6, 1, D_MODEL)

    w_in_bf = w_in.astype(BF16)
    w_out_bf = w_out.astype(BF16)
    pad = LANES - N_GROUPS - N_EXPERTS
    rt_w = jnp.concatenate([rt_group_w, rt_exp_w, jnp.zeros((DEPTH, D_MODEL, pad), F32)], axis=-1)
    rt_b = jnp.concatenate([rt_group_b, rt_exp_b, jnp.zeros((DEPTH, pad), F32)], axis=-1).reshape(DEPTH, 1, LANES)
    sink_b = jnp.broadcast_to(attn_sink[:, :, None], (DEPTH, ATT_HEADS, LANES))
    cos, sin = _rope_tables()
    cache_k4 = cache_k.reshape(DEC_BATCH, DEPTH, PAST_LEN, KV_WIDTH)
    cache_v4 = cache_v.reshape(DEC_BATCH, DEPTH, PAST_LEN, KV_WIDTH)
    s0_lat = _state_to_pairs(state_wkv.transpose(1, 0, 2, 3, 4, 5).reshape(DEPTH * DEC_BATCH, 2, RW_HEADS, HEAD_DIM, HEAD_DIM))
    s0_lat = s0_lat.reshape(DEPTH, DEC_BATCH, 2, 2, HEAD_DIM, LANES)
    s0_ctx = jnp.zeros((BATCH, 2, 2, HEAD_DIM, LANES), F32)
    xs_zero = jnp.zeros((N_SLOTS * D_CHUNKS, LANES), F32)
    rw_prm = (rw_mu, rw_w0, rw_w1, rw_w2, rw_a0, rw_a1, rw_a2, rw_kk, rw_ka, rw_rk)
    out_prm = (conv_w, rw_ln_g.reshape(DEPTH, 1, RW_WIDTH), rw_ln_b.reshape(DEPTH, 1, RW_WIDTH), w_out_bf,
               norm_ffn.reshape(DEPTH, 1, D_MODEL), rt_w, rt_b)

    x = jnp.concatenate([x_prompt.reshape(N_CTX_ROWS, D_MODEL), x_sample.reshape(N_LAT_ROWS, D_MODEL)], axis=0)
    keys, vals, states = [], [], []
    for l in range(DEPTH):
        conv, q, k, v, rw = _inproj(x, mod3, norm_mix, w_in_bf, l)
        cpe, cne = _tile_edges(conv)
        rpe, rne = _tile_edges(rw)
        rpe, rne = rpe[:, :, 0:4 * RW_WIDTH], rne[:, :, 0:4 * RW_WIDTH]

        sc_c, bonus_c = _rwkv_pre(rw, rpe, rne, rw_prm, l, 0, NT_CTX)
        sc_l, bonus_l = _rwkv_pre(rw, rpe, rne, rw_prm, l, NT_CTX, NT - NT_CTX)
        yf_c, yb_c, sfin = _rwkv_scan(sc_c.reshape(2, 6, BATCH, SEQ, RW_WIDTH), s0_ctx, BATCH, SEQ, 4, SEQ, True)
        yf_l, yb_l, _ = _rwkv_scan(sc_l.reshape(2, 6, DEC_BATCH, DEC_SEQ, RW_WIDTH), s0_lat[l], DEC_BATCH,
                                   DEC_SEQ, 4, 256, False)
        yf = jnp.concatenate([yf_c.reshape(N_CTX_ROWS, RW_WIDTH), yf_l.reshape(N_LAT_ROWS, RW_WIDTH)], axis=0)
        yb = jnp.concatenate([yb_c.reshape(N_CTX_ROWS, RW_WIDTH), yb_l.reshape(N_LAT_ROWS, RW_WIDTH)], axis=0)
        bonus = jnp.concatenate([bonus_c, bonus_l], axis=0)

        att_c = _ctx_attention(q, k, v, sink_b, l)
        att_l = _lat_attention(q, k, v, cache_k4, cache_v4, cos, sin, sink_b, l)
        att = jnp.concatenate([att_c, att_l], axis=0)

        x_mid, h3, route, counts = _outproj(x, conv, cpe, cne, att, yf, yb, bonus, rw, mod3, out_prm, l)
        slots, block_exp, n_used = _slot_plan(route, counts)
        xs = _dispatch(slots, h3, xs_zero)
        ys = _experts(block_exp, n_used, xs, exp_w1, exp_w3, exp_w2, l)
        x = _combine(slots, x_mid, route, mod3, norm_out, ys, l, l == DEPTH - 1)

        keys.append(k[:N_CTX_ROWS].reshape(BATCH, SEQ, ATT_KV_HEADS, HEAD_DIM))
        vals.append(v[:N_CTX_ROWS].reshape(BATCH, SEQ, ATT_KV_HEADS, HEAD_DIM))
        states.append(_pairs_to_state(sfin))

    y_prompt = x[:N_CTX_ROWS].reshape(BATCH, SEQ, D_MODEL)
    y_sample = x[N_CTX_ROWS:].reshape(DEC_BATCH, DEC_SEQ, D_MODEL)
    return (y_prompt, y_sample, jnp.stack(keys, axis=1), jnp.stack(vals, axis=1), jnp.stack(states, axis=1))
```

```python
import functools

import jax
import jax.numpy as jnp
from jax import lax
from jax.experimental import pallas as pl
from jax.experimental.pallas import tpu as pltpu

F32 = jnp.float32
BF16 = jnp.bfloat16
HIGHEST = lax.Precision.HIGHEST

D_MODEL = 1024
BATCH = 16
SEQ = 256
DEPTH = 4
DEC_BATCH = 4
DEC_SEQ = 2048
PAST_LEN = 256
GRID_W = 64
HEAD_DIM = 64
CONV_CH = 256
ATT_WIDTH = 512
ATT_HEADS = 8
ATT_KV_HEADS = 2
ATT_GQ = 4
KV_WIDTH = 128
WINDOW = 128
ATT_BLOCK = 128
RW_WIDTH = 256
RW_HEADS = 4
LORA = 64
IN_WIDTH = 2816
N_GROUPS = 4
EXP_PER_GROUP = 8
N_EXPERTS = 32
TOP_K = 2
D_EXPERT = 512
ROPE_BASE = 10000.0
NORM_EPS = 1e-6
GN_EPS = 64e-5
DECAY_SCALE = 0.6065306597126334
NEG_INF = -1e30

LANES = 128
SUBLANES = 8
N_CTX_ROWS = BATCH * SEQ
N_LAT_ROWS = DEC_BATCH * DEC_SEQ
N_ROWS = N_CTX_ROWS + N_LAT_ROWS
TM = 256
NT = N_ROWS // TM
NT_CTX = N_CTX_ROWS // TM
TILES_PER_LAT = DEC_SEQ // TM
N_COND = 8
D_CHUNKS = D_MODEL // LANES
MOE_BLOCK = 256
N_ASG = N_ROWS * TOP_K
N_MOE_BLOCKS = N_ASG // MOE_BLOCK + N_EXPERTS
N_SLOTS = N_MOE_BLOCKS * MOE_BLOCK
VMEM_LIMIT = 56 * 1024 * 1024


def _params(sem, vmem=VMEM_LIMIT):
    return pltpu.CompilerParams(dimension_semantics=sem, vmem_limit_bytes=vmem)


def _cond_of_tile(i):
    return jnp.where(i < NT_CTX, 0, 1 + (i - NT_CTX) // TILES_PER_LAT)


def _mod_spec(layer, j):
    return pl.BlockSpec((1, 1, D_MODEL),
                        lambda i: ((layer * N_COND + _cond_of_tile(i)) * 6 + j, 0, 0))


def _layer_spec(shape, layer):
    nd = len(shape)
    return pl.BlockSpec((None,) + tuple(shape), lambda *_: (layer,) + (0,) * nd)


def _rms(x, g):
    return x * lax.rsqrt(jnp.mean(x * x, axis=-1, keepdims=True) + NORM_EPS) * g


def _bdot(a, b):
    return jnp.dot(a.astype(BF16), b.astype(BF16), preferred_element_type=F32)


def _mod_kernel(c_ref, w_ref, b_ref, o_ref):
    c = c_ref[...]
    s = c * jax.nn.sigmoid(c)
    o_ref[...] = jnp.dot(s, w_ref[...], preferred_element_type=F32, precision=HIGHEST) + b_ref[...]


def _modulation(cond, w_mod, b_mod):
    tn = 1536
    return pl.pallas_call(
        _mod_kernel,
        grid=(DEPTH, 6 * D_MODEL // tn),
        in_specs=[pl.BlockSpec((N_COND, D_MODEL), lambda l, j: (0, 0)),
                  pl.BlockSpec((None, D_MODEL, tn), lambda l, j: (l, 0, j)),
                  pl.BlockSpec((None, 1, tn), lambda l, j: (l, 0, j))],
        out_specs=pl.BlockSpec((None, N_COND, tn), lambda l, j: (l, 0, j)),
        out_shape=jax.ShapeDtypeStruct((DEPTH, N_COND, 6 * D_MODEL), F32),
        compiler_params=_params(("arbitrary", "arbitrary")),
        name="modulation",
    )(cond, w_mod, b_mod.reshape(DEPTH, 1, 6 * D_MODEL))


def _inproj_kernel(x_ref, sh_ref, sc_ref, g_ref, w_ref, conv_ref, q_ref, k_ref, v_ref, rw_ref, cedge_ref, redge_ref):
    h = _rms(x_ref[...], g_ref[...]) * (1.0 + sc_ref[0]) + sh_ref[0]
    p = jnp.dot(h.astype(BF16), w_ref[...], preferred_element_type=F32)
    conv_ref[...] = p[:, 0:768]
    q_ref[...] = p[:, 768:1280]
    k_ref[...] = p[:, 1280:1408]
    v_ref[...] = p[:, 1408:1536]
    rw_ref[...] = p[:, 1536:2816]
    cedge_ref[0, 0:1, :] = p[0:1, 0:768]
    cedge_ref[0, 1:2, :] = p[TM - 1:TM, 0:768]
    redge_ref[0, 0:1, :] = p[0:1, 1536:2560]
    redge_ref[0, 1:2, :] = p[TM - 1:TM, 1536:2560]


def _inproj(x, mod3, norm_mix, w_in_bf, layer):
    widths = (768, ATT_WIDTH, KV_WIDTH, KV_WIDTH, 5 * RW_WIDTH)
    edges = (768, 4 * RW_WIDTH)
    return pl.pallas_call(
        _inproj_kernel,
        grid=(NT,),
        in_specs=[pl.BlockSpec((TM, D_MODEL), lambda i: (i, 0)),
                  _mod_spec(layer, 0), _mod_spec(layer, 1),
                  _layer_spec((1, D_MODEL), layer),
                  _layer_spec((D_MODEL, IN_WIDTH), layer)],
        out_specs=[pl.BlockSpec((TM, w), lambda i: (i, 0)) for w in widths]
        + [pl.BlockSpec((1, 2, w), lambda i: (i, 0, 0)) for w in edges],
        out_shape=[jax.ShapeDtypeStruct((N_ROWS, w), F32) for w in widths]
        + [jax.ShapeDtypeStruct((NT, 2, w), F32) for w in edges],
        compiler_params=_params(("parallel",)),
        name="inproj",
    )(x, mod3, mod3, norm_mix.reshape(DEPTH, 1, D_MODEL), w_in_bf)


def _tile_edges(edge):
    c = edge.shape[-1]
    first, last = edge[:, 0], edge[:, 1]
    t = jnp.arange(NT)
    lat = t >= NT_CTX
    pos = (t - NT_CTX) % TILES_PER_LAT
    has_prev = (lat & (pos != 0))[:, None]
    has_next = (lat & (pos != TILES_PER_LAT - 1))[:, None]
    prev = jnp.where(has_prev, jnp.roll(last, 1, axis=0), 0.0)
    nxt = jnp.where(has_next, jnp.roll(first, -1, axis=0), 0.0)
    return prev.reshape(NT, 1, c), nxt.reshape(NT, 1, c)


def _shift_rows(x, prev_row, next_row):
    n = x.shape[0]
    row = lax.broadcasted_iota(jnp.int32, x.shape, 0)
    before = jnp.where(row == 0, prev_row, pltpu.roll(x, 1, axis=0))
    after = jnp.where(row == n - 1, next_row, pltpu.roll(x, n - 1, axis=0))
    return before, after


def _head_ones(n):
    r = lax.broadcasted_iota(jnp.int32, (n, n), 0) // HEAD_DIM
    c = lax.broadcasted_iota(jnp.int32, (n, n), 1) // HEAD_DIM
    return (r == c).astype(F32)


def _head_sum(x, ones):
    return jnp.dot(x, ones, preferred_element_type=F32, precision=HIGHEST)


def _rwpre_kernel(x_ref, pe_ref, ne_ref, mu_ref, w0_ref, w1_ref, w2_ref, a0_ref, a1_ref, a2_ref,
                  kk_ref, ka_ref, rk_ref, sc_ref, bonus_ref):
    x = x_ref[...]
    before, after = _shift_rows(x, pe_ref[0], ne_ref[0])
    ones = _head_ones(RW_WIDTH)
    bonus = jnp.zeros((TM, RW_WIDTH), F32)
    for d in range(2):
        nbr = before if d == 0 else after
        xd = x + (nbr - x) * mu_ref[d:d + 1, :]
        r = xd[:, 0:256]
        k = xd[:, 256:512]
        v = xd[:, 512:768]
        z = xd[:, 768:1024]
        lw = _bdot(jnp.tanh(_bdot(z, w1_ref[d])), w2_ref[d])
        w = jnp.exp(-DECAY_SCALE * jax.nn.sigmoid(w0_ref[d:d + 1, :] + lw))
        la = _bdot(_bdot(z, a1_ref[d]), a2_ref[d])
        a = jax.nn.sigmoid(a0_ref[d:d + 1, :] + la)
        kk = k * kk_ref[d:d + 1, :]
        kk = kk / jnp.maximum(jnp.sqrt(_head_sum(kk * kk, ones)), 1e-12)
        k2 = k * (1.0 + (a - 1.0) * ka_ref[d:d + 1, :])
        bonus = bonus + _head_sum(r * k2 * rk_ref[d:d + 1, :], ones) * v
        sc_ref[d, 0] = r
        sc_ref[d, 1] = w
        sc_ref[d, 2] = k2
        sc_ref[d, 3] = v
        sc_ref[d, 4] = -kk
        sc_ref[d, 5] = kk * a
    bonus_ref[...] = bonus


def _rwkv_pre(rw, prev_e, next_e, prm, layer, tile0, ntiles):
    (mu, w0, w1, w2, a0, a1, a2, kk, ka, rk) = prm
    nrows = ntiles * TM
    return pl.pallas_call(
        _rwpre_kernel,
        grid=(ntiles,),
        in_specs=[pl.BlockSpec((TM, 4 * RW_WIDTH), lambda i: (i + tile0, 0)),
                  pl.BlockSpec((1, 1, 4 * RW_WIDTH), lambda i: (i + tile0, 0, 0)),
                  pl.BlockSpec((1, 1, 4 * RW_WIDTH), lambda i: (i + tile0, 0, 0)),
                  _layer_spec((2, 4 * RW_WIDTH), layer),
                  _layer_spec((2, RW_WIDTH), layer),
                  _layer_spec((2, RW_WIDTH, LORA), layer),
                  _layer_spec((2, LORA, RW_WIDTH), layer),
                  _layer_spec((2, RW_WIDTH), layer),
                  _layer_spec((2, RW_WIDTH, LORA), layer),
                  _layer_spec((2, LORA, RW_WIDTH), layer),
                  _layer_spec((2, RW_WIDTH), layer),
                  _layer_spec((2, RW_WIDTH), layer),
                  _layer_spec((2, RW_WIDTH), layer)],
        out_specs=[pl.BlockSpec((2, 6, TM, RW_WIDTH), lambda i: (0, 0, i, 0)),
                   pl.BlockSpec((TM, RW_WIDTH), lambda i: (i, 0))],
        out_shape=[jax.ShapeDtypeStruct((2, 6, nrows, RW_WIDTH), F32),
                   jax.ShapeDtypeStruct((nrows, RW_WIDTH), F32)],
        compiler_params=_params(("parallel",)),
        name="rwkv_pre",
    )(rw, prev_e, next_e, mu, w0, w1, w2, a0, a1, a2, kk, ka, rk)


def _half_ones():
    r = lax.broadcasted_iota(jnp.int32, (LANES, LANES), 0) // HEAD_DIM
    c = lax.broadcasted_iota(jnp.int32, (LANES, LANES), 1) // HEAD_DIM
    return (r == c).astype(BF16)


def _scan_kernel(lf_ref, lb_ref, c_ref, s0_ref, ylf_ref, ylb_ref, ycf_ref, ycb_ref, sfin_ref, stl_ref, stc_ref,
                 *, n_lat, n_ctx, tblk):
    tb = pl.program_id(0)

    @pl.when(tb == 0)
    def _():
        stl_ref[...] = s0_ref[...]

    stc_ref[...] = jnp.zeros_like(stc_ref)

    ones = _half_ones()
    vi = lax.broadcasted_iota(jnp.int32, (HEAD_DIM, LANES), 0)
    li = lax.broadcasted_iota(jnp.int32, (HEAD_DIM, LANES), 1)
    diag = vi == (li % HEAD_DIM)
    diag_bf = diag.astype(BF16)
    r_row, w_row, k_row, v_row, a_row, b_row = range(6)
    seqs = [("lat", b, d) for b in range(n_lat) for d in range(2)] + \
           [("ctx", b, d) for b in range(n_ctx) for d in range(2)]
    chains = [(q, p) for q in seqs for p in range(2)]

    def row_of(t, d):
        return t if d == 0 else tblk - 1 - t

    def in_row(q, j, t):
        kind, b, d = q
        if kind == "lat":
            return (lf_ref if d == 0 else lb_ref)[j, b, pl.ds(row_of(t, d), 1), :]
        return c_ref[d, j, b, pl.ds(row_of(t, d), 1), :]

    def y_ref_of(q):
        kind, _, d = q
        if kind == "lat":
            return ylf_ref if d == 0 else ylb_ref
        return ycf_ref if d == 0 else ycb_ref

    def state_ref(q):
        return stl_ref if q[0] == "lat" else stc_ref

    def load_rows(t, which):
        rows = {}
        for q in seqs:
            full = {j: in_row(q, j, t) for j in which}
            for p in range(2):
                rows[(q, p)] = {j: x[:, p * LANES:(p + 1) * LANES] for j, x in full.items()}
        return rows

    def states():
        return {(q, p): state_ref(q)[q[1], q[2], p] for (q, p) in chains}

    def emit_y(t, ss, rows):
        pr = jnp.concatenate([(ss[c] * rows[c][r_row]).astype(BF16) for c in chains], axis=0)
        ybc = jnp.dot(pr, ones, preferred_element_type=F32)
        ys = {}
        for i, c in enumerate(chains):
            blk = ybc[i * HEAD_DIM:(i + 1) * HEAD_DIM]
            ys[c] = jnp.sum(jnp.where(diag, blk, 0.0), axis=0, keepdims=True)
        for q in seqs:
            y_ref_of(q)[q[1], pl.ds(row_of(t, q[2]), 1), :] = jnp.concatenate([ys[(q, 0)], ys[(q, 1)]], axis=1)

    def step(t, carry):
        rows = load_rows(t, (w_row, k_row, v_row, a_row, b_row))
        tp = jnp.maximum(t - 1, 0)
        ss = states()
        emit_y(tp, ss, load_rows(tp, (r_row,)))
        pa = jnp.concatenate([(ss[c] * rows[c][a_row]).astype(BF16) for c in chains], axis=0)
        dv = jnp.concatenate([diag_bf * rows[c][v_row].astype(BF16) for c in chains], axis=0)
        sa = jnp.dot(pa, ones, preferred_element_type=F32)
        vc = jnp.dot(dv, ones, preferred_element_type=F32)
        for i, c in enumerate(chains):
            (q, p), rw = c, rows[c]
            sl = slice(i * HEAD_DIM, (i + 1) * HEAD_DIM)
            state_ref(q)[q[1], q[2], p] = ss[c] * rw[w_row] + sa[sl] * rw[b_row] + vc[sl] * rw[k_row]
        return carry

    lax.fori_loop(0, tblk, step, 0)
    emit_y(tblk - 1, states(), load_rows(tblk - 1, (r_row,)))
    sfin_ref[...] = stc_ref[...]


def _rwkv_scan(sc_lat, sc_ctx, s0_lat, n_lat, lat_len, n_ctx_total, tblk):
    ntb = lat_len // tblk
    n_ctx = n_ctx_total // ntb
    kern = functools.partial(_scan_kernel, n_lat=n_lat, n_ctx=n_ctx, tblk=tblk)
    st_shape = (2, 2, HEAD_DIM, LANES)
    return pl.pallas_call(
        kern,
        grid=(ntb,),
        in_specs=[pl.BlockSpec((None, 6, n_lat, tblk, RW_WIDTH), lambda t: (0, 0, 0, t, 0)),
                  pl.BlockSpec((None, 6, n_lat, tblk, RW_WIDTH), lambda t: (1, 0, 0, ntb - 1 - t, 0)),
                  pl.BlockSpec((2, 6, n_ctx, tblk, RW_WIDTH), lambda t: (0, 0, t, 0, 0)),
                  pl.BlockSpec((n_lat,) + st_shape, lambda t: (0, 0, 0, 0, 0))],
        out_specs=[pl.BlockSpec((n_lat, tblk, RW_WIDTH), lambda t: (0, t, 0)),
                   pl.BlockSpec((n_lat, tblk, RW_WIDTH), lambda t: (0, ntb - 1 - t, 0)),
                   pl.BlockSpec((n_ctx, tblk, RW_WIDTH), lambda t: (t, 0, 0)),
                   pl.BlockSpec((n_ctx, tblk, RW_WIDTH), lambda t: (t, 0, 0)),
                   pl.BlockSpec((n_ctx,) + st_shape, lambda t: (t, 0, 0, 0, 0))],
        out_shape=[jax.ShapeDtypeStruct((n_lat, lat_len, RW_WIDTH), F32),
                   jax.ShapeDtypeStruct((n_lat, lat_len, RW_WIDTH), F32),
                   jax.ShapeDtypeStruct((n_ctx_total, tblk, RW_WIDTH), F32),
                   jax.ShapeDtypeStruct((n_ctx_total, tblk, RW_WIDTH), F32),
                   jax.ShapeDtypeStruct((n_ctx_total,) + st_shape, F32)],
        scratch_shapes=[pltpu.VMEM((n_lat,) + st_shape, F32), pltpu.VMEM((n_ctx,) + st_shape, F32)],
        compiler_params=_params(("arbitrary",)),
        name="rwkv_scan",
    )(sc_lat, sc_lat, sc_ctx, s0_lat)


def _state_to_pairs(s):
    b = s.shape[0]
    return s.reshape(b, 2, 2, 2, HEAD_DIM, HEAD_DIM).transpose(0, 1, 2, 4, 3, 5).reshape(b, 2, 2, HEAD_DIM, LANES)


def _pairs_to_state(s):
    b = s.shape[0]
    return s.reshape(b, 2, 2, HEAD_DIM, 2, HEAD_DIM).transpose(0, 1, 2, 4, 3, 5).reshape(b, 2, RW_HEADS, HEAD_DIM, HEAD_DIM)


def _rope(x, cos, sin):
    n = x.shape[1]
    m = n // LANES
    if m > 1:
        cos = jnp.concatenate([cos] * m, axis=1)
        sin = jnp.concatenate([sin] * m, axis=1)
    lane = lax.broadcasted_iota(jnp.int32, x.shape, 1)
    swapped = jnp.where((lane % 32) < 16, pltpu.roll(x, n - 16, axis=1), pltpu.roll(x, 16, axis=1))
    return x * cos + swapped * sin


def _attend(q, keys, vals, biases, sink_ref, o_ref):
    nq = q.shape[0]
    qb = (q * (HEAD_DIM ** -0.5)).astype(BF16)
    kbs = [k.astype(BF16) for k in keys]
    vbs = [v.astype(BF16) for v in vals]
    tiled = [None if b is None else jnp.concatenate([b] * ATT_GQ, axis=0) for b in biases]
    for hk in range(ATT_KV_HEADS):
        heads = range(hk * ATT_GQ, (hk + 1) * ATT_GQ)
        kv = slice(hk * HEAD_DIM, (hk + 1) * HEAD_DIM)
        qg = jnp.concatenate([qb[:, h * HEAD_DIM:(h + 1) * HEAD_DIM] for h in heads], axis=0)
        sink = jnp.concatenate([jnp.broadcast_to(sink_ref[h:h + 1, 0:1], (nq, 1)) for h in heads], axis=0)
        ss = []
        for kb, bias in zip(kbs, tiled):
            s = lax.dot_general(qg, kb[:, kv], (((1,), (1,)), ((), ())), preferred_element_type=F32)
            ss.append(s if bias is None else s + bias)
        m_lanes = ss[0]
        for s in ss[1:]:
            m_lanes = jnp.maximum(m_lanes, s)
        m = jnp.maximum(jnp.max(m_lanes, axis=-1, keepdims=True), sink)
        l_lanes = None
        acc = jnp.zeros((ATT_GQ * nq, HEAD_DIM), F32)
        for s, vb in zip(ss, vbs):
            p = jnp.exp(s - m)
            l_lanes = p if l_lanes is None else l_lanes + p
            acc = acc + jnp.dot(p.astype(BF16), vb[:, kv], preferred_element_type=F32)
        out = acc / (jnp.sum(l_lanes, axis=-1, keepdims=True) + jnp.exp(sink - m))
        for g, h in enumerate(heads):
            o_ref[:, h * HEAD_DIM:(h + 1) * HEAD_DIM] = out[g * nq:(g + 1) * nq]


def _ctx_attn_kernel(q_ref, k_ref, v_ref, sink_ref, o_ref):
    chunks = range(0, SEQ, ATT_BLOCK)
    _attend(q_ref[...], [k_ref[c:c + ATT_BLOCK, :] for c in chunks], [v_ref[c:c + ATT_BLOCK, :] for c in chunks],
            [None for _ in chunks], sink_ref, o_ref)


def _ctx_attention(q, k, v, sink_b, layer):
    nq = SEQ // ATT_BLOCK
    return pl.pallas_call(
        _ctx_attn_kernel,
        grid=(BATCH, nq),
        in_specs=[pl.BlockSpec((ATT_BLOCK, ATT_WIDTH), lambda b, i: (b * nq + i, 0)),
                  pl.BlockSpec((SEQ, KV_WIDTH), lambda b, i: (b, 0)),
                  pl.BlockSpec((SEQ, KV_WIDTH), lambda b, i: (b, 0)),
                  pl.BlockSpec((None, ATT_HEADS, LANES), lambda b, i: (layer, 0, 0))],
        out_specs=pl.BlockSpec((ATT_BLOCK, ATT_WIDTH), lambda b, i: (b * nq + i, 0)),
        out_shape=jax.ShapeDtypeStruct((N_CTX_ROWS, ATT_WIDTH), F32),
        compiler_params=_params(("parallel", "parallel")),
        name="ctx_attention",
    )(q, k, v, sink_b)


def _lat_attn_kernel(q_ref, k_ref, v_ref, ck_ref, cv_ref, cos_ref, sin_ref, sink_ref, o_ref):
    n = pl.program_id(1)
    nb = pl.num_programs(1)
    r0 = pl.multiple_of(n * ATT_BLOCK, ATT_BLOCK)
    q = _rope(q_ref[...], cos_ref[pl.ds(r0, ATT_BLOCK), :], sin_ref[pl.ds(r0, ATT_BLOCK), :])
    qi = lax.broadcasted_iota(jnp.int32, (ATT_BLOCK, ATT_BLOCK), 0)
    ki = lax.broadcasted_iota(jnp.int32, (ATT_BLOCK, ATT_BLOCK), 1)
    keys, vals, masks = [], [], []
    for j in (-1, 0, 1):
        blk = jnp.clip(n + j, 0, nb - 1)
        k0 = pl.multiple_of(blk * ATT_BLOCK, ATT_BLOCK)
        kb = _rope(k_ref[pl.ds(k0, ATT_BLOCK), :], cos_ref[pl.ds(k0, ATT_BLOCK), :],
                   sin_ref[pl.ds(k0, ATT_BLOCK), :])
        keys.append(kb)
        vals.append(v_ref[pl.ds(k0, ATT_BLOCK), :])
        rel = ki + j * ATT_BLOCK - qi
        reach = jnp.where((n + j >= 0) & (n + j < nb), WINDOW, -1)
        masks.append(jnp.where(jnp.abs(rel) <= reach, 0.0, NEG_INF))
    for c in range(0, PAST_LEN, ATT_BLOCK):
        keys.append(ck_ref[c:c + ATT_BLOCK, :])
        vals.append(cv_ref[c:c + ATT_BLOCK, :])
        masks.append(None)
    _attend(q, keys, vals, masks, sink_ref, o_ref)


def _lat_attention(q, k, v, cache_k4, cache_v4, cos, sin, sink_b, layer):
    nb = DEC_SEQ // ATT_BLOCK
    qoff = N_CTX_ROWS // ATT_BLOCK
    soff = N_CTX_ROWS // DEC_SEQ
    return pl.pallas_call(
        _lat_attn_kernel,
        grid=(DEC_BATCH, nb),
        in_specs=[pl.BlockSpec((ATT_BLOCK, ATT_WIDTH), lambda b, i: (qoff + b * nb + i, 0)),
                  pl.BlockSpec((DEC_SEQ, KV_WIDTH), lambda b, i: (soff + b, 0)),
                  pl.BlockSpec((DEC_SEQ, KV_WIDTH), lambda b, i: (soff + b, 0)),
                  pl.BlockSpec((None, None, PAST_LEN, KV_WIDTH), lambda b, i: (b, layer, 0, 0)),
                  pl.BlockSpec((None, None, PAST_LEN, KV_WIDTH), lambda b, i: (b, layer, 0, 0)),
                  pl.BlockSpec((DEC_SEQ, LANES), lambda b, i: (0, 0)),
                  pl.BlockSpec((DEC_SEQ, LANES), lambda b, i: (0, 0)),
                  pl.BlockSpec((None, ATT_HEADS, LANES), lambda b, i: (layer, 0, 0))],
        out_specs=pl.BlockSpec((ATT_BLOCK, ATT_WIDTH), lambda b, i: (b * nb + i, 0)),
        out_shape=jax.ShapeDtypeStruct((N_LAT_ROWS, ATT_WIDTH), F32),
        compiler_params=_params(("parallel", "parallel")),
        name="lat_attention",
    )(q, k, v, cache_k4, cache_v4, cos, sin, sink_b)


def _rope_tables():
    pos = jnp.arange(DEC_SEQ)
    rows = (pos // GRID_W).astype(F32)
    cols = (pos % GRID_W).astype(F32)
    inv = ROPE_BASE ** (-jnp.arange(0, 32, 2, dtype=F32) / 32)
    lane = jnp.arange(LANES)
    posl = jnp.where(((lane % HEAD_DIM) // 32 == 0)[None, :], rows[:, None], cols[:, None])
    ang = posl * inv[lane % 16][None, :]
    sign = jnp.where((lane % 32) < 16, -1.0, 1.0)[None, :]
    return jnp.cos(ang), jnp.sin(ang) * sign


def _outproj_kernel(x_ref, conv_ref, cpe_ref, cne_ref, attc_ref, attl_ref, yfc_ref, yfl_ref, ybc_ref, ybl_ref,
                    bonc_ref, bonl_ref, g_ref,
                    cw_ref, lng_ref, lnb_ref, wout_ref, gate_ref, sh_ref, sc_ref, nf_ref, rtw_ref, rtb_ref,
                    xo_ref, h3_ref, route_ref, counts_ref, cnt_ref):
    is_ctx = pl.program_id(0) < NT_CTX
    att = jnp.where(is_ctx, attc_ref[...], attl_ref[...])
    ysum = jnp.where(is_ctx, yfc_ref[...] + ybc_ref[...], yfl_ref[...] + ybl_ref[...])
    bonus = jnp.where(is_ctx, bonc_ref[...], bonl_ref[...])
    conv = conv_ref[...]
    u = conv[:, 256:512] * conv[:, 512:768]
    pe = cpe_ref[0]
    ne = cne_ref[0]
    before, after = _shift_rows(u, pe[:, 256:512] * pe[:, 512:768], ne[:, 256:512] * ne[:, 512:768])
    y_conv = conv[:, 0:256] * (before * cw_ref[0:1, :] + u * cw_ref[1:2, :] + after * cw_ref[2:3, :])

    ones = _head_ones(RW_WIDTH)
    inv_hd = 1.0 / HEAD_DIM
    mean = _head_sum(ysum, ones) * inv_hd
    cen = ysum - mean
    var = _head_sum(cen * cen, ones) * inv_hd
    yn = cen * lax.rsqrt(var + GN_EPS) * lng_ref[...] + lnb_ref[...]
    y_rw = (yn + bonus) * jax.nn.sigmoid(g_ref[...])

    mix = (jnp.dot(y_conv.astype(BF16), wout_ref[0:256, :], preferred_element_type=F32)
           + jnp.dot(att.astype(BF16), wout_ref[256:768, :], preferred_element_type=F32)
           + jnp.dot(y_rw.astype(BF16), wout_ref[768:1024, :], preferred_element_type=F32))
    x = x_ref[...] + gate_ref[0] * mix
    xo_ref[...] = x
    h2 = _rms(x, nf_ref[...]) * (1.0 + sc_ref[0]) + sh_ref[0]
    for j in range(D_CHUNKS):
        h3_ref[pl.ds(j, TM, stride=D_CHUNKS), :] = h2[:, j * LANES:(j + 1) * LANES]

    logits = jnp.dot(h2, rtw_ref[...], preferred_element_type=F32, precision=HIGHEST) + rtb_ref[...]
    lane = lax.broadcasted_iota(jnp.int32, logits.shape, 1)
    big = jnp.int32(1 << 20)
    is_g = lane < N_GROUPS
    gmax = jnp.max(jnp.where(is_g, logits, NEG_INF), axis=-1, keepdims=True)
    gsum = jnp.sum(jnp.where(is_g, jnp.exp(logits - gmax), 0.0), axis=-1, keepdims=True)
    g_top = 1.0 / gsum
    g_idx = jnp.min(jnp.where(is_g & (logits == gmax), lane, big), axis=-1, keepdims=True)
    lo = N_GROUPS + g_idx * EXP_PER_GROUP
    in_grp = (lane >= lo) & (lane < lo + EXP_PER_GROUP)
    emax = jnp.max(jnp.where(in_grp, logits, NEG_INF), axis=-1, keepdims=True)
    pe_ = jnp.where(in_grp, jnp.exp(logits - emax), -1.0)
    p1 = jnp.max(pe_, axis=-1, keepdims=True)
    l1 = jnp.min(jnp.where(pe_ == p1, lane, big), axis=-1, keepdims=True)
    pe2 = jnp.where(lane == l1, -1.0, pe_)
    p2 = jnp.max(pe2, axis=-1, keepdims=True)
    l2 = jnp.min(jnp.where(pe2 == p2, lane, big), axis=-1, keepdims=True)
    wsum = p1 + p2
    w1 = g_top * p1 / wsum
    w2 = g_top * p2 / wsum
    e1 = (l1 - N_GROUPS).astype(F32)
    e2 = (l2 - N_GROUPS).astype(F32)

    @pl.when(pl.program_id(0) == 0)
    def _():
        cnt_ref[...] = jnp.zeros_like(cnt_ref)

    hit1 = lane == l1 - N_GROUPS
    hit2 = lane == l2 - N_GROUPS
    onehot = jnp.where(hit1 | hit2, 1.0, 0.0)
    ri = lax.broadcasted_iota(jnp.int32, (TM, TM), 0)
    ci = lax.broadcasted_iota(jnp.int32, (TM, TM), 1)
    earlier = jnp.where(ci < ri, 1.0, 0.0).astype(BF16)
    before = jnp.dot(earlier, onehot.astype(BF16), preferred_element_type=F32) + cnt_ref[...]
    r1 = jnp.sum(jnp.where(hit1, before, 0.0), axis=-1, keepdims=True)
    r2 = jnp.sum(jnp.where(hit2, before, 0.0), axis=-1, keepdims=True)
    cnt_ref[...] = cnt_ref[...] + jnp.sum(onehot, axis=0, keepdims=True)
    counts_ref[...] = jnp.broadcast_to(cnt_ref[...], counts_ref.shape)

    route = jnp.zeros_like(logits)
    for j, col in enumerate((e1, e2, w1, w2, r1, r2)):
        route = jnp.where(lane == j, col, route)
    route_ref[...] = route


def _outproj(x, conv, cpe, cne, att, yf, yb, bonus, rw, mod3, prm, layer):
    conv_w, ln_g, ln_b, w_out_bf, norm_ffn, rt_w, rt_b = prm
    row = lambda w: pl.BlockSpec((TM, w), lambda i: (i, 0))
    ctx = lambda w: pl.BlockSpec((TM, w), lambda i: (jnp.minimum(i, NT_CTX - 1), 0))
    lat = lambda w: pl.BlockSpec((TM, w), lambda i: (jnp.maximum(i - NT_CTX, 0), 0))
    edge = pl.BlockSpec((1, 1, 768), lambda i: (i, 0, 0))
    return pl.pallas_call(
        _outproj_kernel,
        grid=(NT,),
        in_specs=[row(D_MODEL), row(768), edge, edge, ctx(ATT_WIDTH), lat(ATT_WIDTH), ctx(RW_WIDTH), lat(RW_WIDTH),
                  ctx(RW_WIDTH), lat(RW_WIDTH), ctx(RW_WIDTH), lat(RW_WIDTH),
                  pl.BlockSpec((TM, RW_WIDTH), lambda i: (i, 4)),
                  _layer_spec((3, CONV_CH), layer), _layer_spec((1, RW_WIDTH), layer),
                  _layer_spec((1, RW_WIDTH), layer), _layer_spec((D_MODEL, D_MODEL), layer),
                  _mod_spec(layer, 2), _mod_spec(layer, 3), _mod_spec(layer, 4),
                  _layer_spec((1, D_MODEL), layer), _layer_spec((D_MODEL, LANES), layer),
                  _layer_spec((1, LANES), layer)],
        out_specs=[row(D_MODEL), pl.BlockSpec((TM * D_CHUNKS, LANES), lambda i: (i, 0)), row(LANES),
                   pl.BlockSpec((SUBLANES, LANES), lambda i: (0, 0))],
        out_shape=[jax.ShapeDtypeStruct((N_ROWS, D_MODEL), F32),
                   jax.ShapeDtypeStruct((N_ROWS * D_CHUNKS, LANES), F32),
                   jax.ShapeDtypeStruct((N_ROWS, LANES), F32),
                   jax.ShapeDtypeStruct((SUBLANES, LANES), F32)],
        scratch_shapes=[pltpu.VMEM((1, LANES), F32)],
        compiler_params=_params(("arbitrary",)),
        name="outproj_router",
    )(x, conv, cpe, cne, *att, *yf, *yb, *bonus, rw, conv_w, ln_g, ln_b, w_out_bf, mod3, mod3, mod3,
      norm_ffn, rt_w, rt_b)


def _slot_plan(route, counts):
    e = route[:, 0:2].astype(jnp.int32)
    rank = route[:, 4:6].astype(jnp.int32)
    counts = counts[0, 0:N_EXPERTS].astype(jnp.int32)
    padded = (counts + MOE_BLOCK - 1) // MOE_BLOCK * MOE_BLOCK
    pad_end = jnp.cumsum(padded)
    pad_start = pad_end - padded
    slots = (pad_start[e] + rank).astype(jnp.int32)
    blk0 = jnp.arange(N_MOE_BLOCKS, dtype=jnp.int32) * MOE_BLOCK
    block_exp = jnp.sum((pad_end[None, :] <= blk0[:, None]).astype(jnp.int32), axis=1)
    block_exp = jnp.minimum(block_exp, N_EXPERTS - 1)
    n_used = (pad_end[-1] // MOE_BLOCK).astype(jnp.int32)
    present = jnp.where(counts > 0, jnp.arange(N_EXPERTS, dtype=jnp.int32), N_EXPERTS)
    later = jnp.flip(lax.cummin(jnp.flip(present)))
    next_expert = jnp.concatenate([later[1:], jnp.full((1,), N_EXPERTS, jnp.int32)])
    return slots.reshape(-1), block_exp, n_used.reshape(1), next_expert


def _dispatch_kernel(slots_ref, h3_ref, xs_in_ref, xs_ref, sem):
    del xs_in_ref
    i = pl.program_id(0)
    base = i * (TM * TOP_K)

    def row_copy(r, s):
        src = h3_ref.at[pl.ds(pl.multiple_of(r * D_CHUNKS, D_CHUNKS), D_CHUNKS)]
        dst = xs_ref.at[pl.ds(pl.multiple_of(s * D_CHUNKS, D_CHUNKS), D_CHUNKS)]
        return pltpu.make_async_copy(src, dst, sem)

    def issue(r, c):
        for k in range(TOP_K):
            row_copy(r, slots_ref[base + r * TOP_K + k]).start(priority=k)
        return c

    lax.fori_loop(0, TM, issue, 0)

    def drain(r, c):
        for k in range(TOP_K):
            row_copy(0, 0).wait()
        return c

    lax.fori_loop(0, TM, drain, 0)


def _dispatch(slots, h3, xs_zero):
    return pl.pallas_call(
        _dispatch_kernel,
        grid_spec=pltpu.PrefetchScalarGridSpec(
            num_scalar_prefetch=1,
            grid=(NT,),
            in_specs=[pl.BlockSpec((TM * D_CHUNKS, LANES), lambda i, s: (i, 0)),
                      pl.BlockSpec(memory_space=pl.ANY)],
            out_specs=pl.BlockSpec(memory_space=pl.ANY),
            scratch_shapes=[pltpu.SemaphoreType.DMA(())]),
        out_shape=jax.ShapeDtypeStruct((N_SLOTS * D_CHUNKS, LANES), F32),
        input_output_aliases={2: 0},
        compiler_params=_params(("arbitrary",)),
        name="moe_dispatch",
    )(slots, h3, xs_zero)


def _expert_kernel(be_ref, nu_ref, nxt_ref, xs_ref, w1_hbm, w3_hbm, w2_hbm, ys_ref,
                   w1f, w3f, w2f, w1b, w3b, w2b, sems, turn_ref, *, layer):
    i = pl.program_id(0)
    n_used = nu_ref[0]
    blk = jnp.minimum(i, n_used - 1)
    expert = be_ref[blk]
    prev_expert = be_ref[jnp.maximum(blk - 1, 0)]

    def fetch(e, slot):
        return [pltpu.make_async_copy(hbm.at[layer, e], buf.at[slot], sems.at[slot])
                for hbm, buf in ((w1_hbm, w1f), (w3_hbm, w3f), (w2_hbm, w2f))]

    @pl.when(i == 0)
    def _():
        turn_ref[0] = 0
        for cp in fetch(expert, 0):
            cp.start()

    @pl.when((i < n_used) & ((i == 0) | (expert != prev_expert)))
    def _():
        slot = turn_ref[0] % 2
        for cp in fetch(expert, slot):
            cp.wait()
        nxt = nxt_ref[expert]

        @pl.when(nxt < N_EXPERTS)
        def _():
            for cp in fetch(nxt, 1 - slot):
                cp.start()

        w1b[...] = w1f[slot].astype(BF16)
        w3b[...] = w3f[slot].astype(BF16)
        w2b[...] = w2f[slot].astype(BF16)
        turn_ref[0] = turn_ref[0] + 1

    @pl.when(i < n_used)
    def _():
        x = jnp.concatenate([xs_ref[pl.ds(j, MOE_BLOCK, stride=D_CHUNKS), :].astype(BF16)
                             for j in range(D_CHUNKS)], axis=1)
        a = jnp.dot(x, w1b[...], preferred_element_type=F32)
        b = jnp.dot(x, w3b[...], preferred_element_type=F32)
        hid = (a * jax.nn.sigmoid(a) * b).astype(BF16)
        y = jnp.dot(hid, w2b[...], preferred_element_type=F32)
        for j in range(D_CHUNKS):
            ys_ref[pl.ds(j, MOE_BLOCK, stride=D_CHUNKS), :] = y[:, j * LANES:(j + 1) * LANES]

    @pl.when(i >= n_used)
    def _():
        ys_ref[...] = jnp.zeros_like(ys_ref)


def _experts(block_exp, n_used, next_expert, xs, w1, w3, w2, layer):
    rows = pl.BlockSpec((MOE_BLOCK * D_CHUNKS, LANES), lambda i, be, nu, nx: (i, 0))
    hbm = pl.BlockSpec(memory_space=pl.ANY)
    return pl.pallas_call(
        functools.partial(_expert_kernel, layer=layer),
        grid_spec=pltpu.PrefetchScalarGridSpec(
            num_scalar_prefetch=3,
            grid=(N_MOE_BLOCKS,),
            in_specs=[rows, hbm, hbm, hbm],
            out_specs=rows,
            scratch_shapes=[pltpu.VMEM((2, D_MODEL, D_EXPERT), F32), pltpu.VMEM((2, D_MODEL, D_EXPERT), F32),
                            pltpu.VMEM((2, D_EXPERT, D_MODEL), F32),
                            pltpu.VMEM((D_MODEL, D_EXPERT), BF16), pltpu.VMEM((D_MODEL, D_EXPERT), BF16),
                            pltpu.VMEM((D_EXPERT, D_MODEL), BF16),
                            pltpu.SemaphoreType.DMA((2,)), pltpu.SMEM((1,), jnp.int32)]),
        out_shape=jax.ShapeDtypeStruct((N_SLOTS * D_CHUNKS, LANES), F32),
        compiler_params=_params(("arbitrary",)),
        name="moe_experts",
    )(block_exp, n_used, next_expert, xs, w1, w3, w2)


def _combine_kernel(slots_ref, x_ref, route_ref, gate_ref, g_ref, ys_ref, o_ref, buf, sem, *, final_norm):
    i = pl.program_id(0)
    base = i * (TM * TOP_K)

    def row_copy(s, k, r):
        src = ys_ref.at[pl.ds(pl.multiple_of(s * D_CHUNKS, D_CHUNKS), D_CHUNKS)]
        dst = buf.at[k, pl.ds(pl.multiple_of(r * D_CHUNKS, D_CHUNKS), D_CHUNKS)]
        return pltpu.make_async_copy(src, dst, sem)

    def issue(r, c):
        for k in range(TOP_K):
            row_copy(slots_ref[base + r * TOP_K + k], k, r).start(priority=k)
        return c

    lax.fori_loop(0, TM, issue, 0)

    def drain(r, c):
        for k in range(TOP_K):
            row_copy(0, k, 0).wait()
        return c

    lax.fori_loop(0, TM, drain, 0)
    route = route_ref[...]
    w1 = route[:, 2:3]
    w2 = route[:, 3:4]
    y = jnp.concatenate([w1 * buf[0, pl.ds(j, TM, stride=D_CHUNKS), :] + w2 * buf[1, pl.ds(j, TM, stride=D_CHUNKS), :]
                         for j in range(D_CHUNKS)], axis=1)
    x = x_ref[...] + gate_ref[0] * y
    if final_norm:
        x = _rms(x, g_ref[...])
    o_ref[...] = x


def _combine(slots, x, route, mod3, norm_out, ys, layer, final_norm):
    kern = functools.partial(_combine_kernel, final_norm=final_norm)
    return pl.pallas_call(
        kern,
        grid_spec=pltpu.PrefetchScalarGridSpec(
            num_scalar_prefetch=1,
            grid=(NT,),
            in_specs=[pl.BlockSpec((TM, D_MODEL), lambda i, s: (i, 0)),
                      pl.BlockSpec((TM, LANES), lambda i, s: (i, 0)),
                      pl.BlockSpec((1, 1, D_MODEL),
                                   lambda i, s: ((layer * N_COND + _cond_of_tile(i)) * 6 + 5, 0, 0)),
                      pl.BlockSpec((1, D_MODEL), lambda i, s: (0, 0)),
                      pl.BlockSpec(memory_space=pl.ANY)],
            out_specs=pl.BlockSpec((TM, D_MODEL), lambda i, s: (i, 0)),
            scratch_shapes=[pltpu.VMEM((TOP_K, TM * D_CHUNKS, LANES), F32),
                            pltpu.SemaphoreType.DMA(())]),
        out_shape=jax.ShapeDtypeStruct((N_ROWS, D_MODEL), F32),
        compiler_params=_params(("arbitrary",)),
        name="moe_combine",
    )(slots, x, route, mod3, norm_out.reshape(1, D_MODEL), ys)


def kernel(x_prompt, x_sample, cache_k, cache_v, state_wkv, c, c_ctx, w_mod, b_mod, norm_mix, norm_ffn,
           norm_out, w_in, w_out, conv_w, attn_sink, rw_mu, rw_w0, rw_w1, rw_w2, rw_a0, rw_a1, rw_a2,
           rw_kk, rw_ka, rw_rk, rw_ln_g, rw_ln_b, rt_group_w, rt_group_b, rt_exp_w, rt_exp_b,
           exp_w1, exp_w3, exp_w2):
    cond = jnp.concatenate([c_ctx[None, :], c, jnp.zeros((N_COND - 1 - DEC_BATCH, D_MODEL), F32)], axis=0)
    mod3 = _modulation(cond, w_mod, b_mod).reshape(DEPTH * N_COND * 6, 1, D_MODEL)

    w_in_bf = w_in.astype(BF16)
    w_out_bf = w_out.astype(BF16)
    pad = LANES - N_GROUPS - N_EXPERTS
    rt_w = jnp.concatenate([rt_group_w, rt_exp_w, jnp.zeros((DEPTH, D_MODEL, pad), F32)], axis=-1)
    rt_b = jnp.concatenate([rt_group_b, rt_exp_b, jnp.zeros((DEPTH, pad), F32)], axis=-1).reshape(DEPTH, 1, LANES)
    sink_b = jnp.broadcast_to(attn_sink[:, :, None], (DEPTH, ATT_HEADS, LANES))
    cos, sin = _rope_tables()
    cache_k4 = cache_k.reshape(DEC_BATCH, DEPTH, PAST_LEN, KV_WIDTH)
    cache_v4 = cache_v.reshape(DEC_BATCH, DEPTH, PAST_LEN, KV_WIDTH)
    s0_lat = _state_to_pairs(state_wkv.transpose(1, 0, 2, 3, 4, 5).reshape(DEPTH * DEC_BATCH, 2, RW_HEADS, HEAD_DIM, HEAD_DIM))
    s0_lat = s0_lat.reshape(DEPTH, DEC_BATCH, 2, 2, HEAD_DIM, LANES)
    xs_zero = jnp.zeros((N_SLOTS * D_CHUNKS, LANES), F32)
    rw_prm = (rw_mu, rw_w0, rw_w1, rw_w2, rw_a0, rw_a1, rw_a2, rw_kk, rw_ka, rw_rk)
    out_prm = (conv_w, rw_ln_g.reshape(DEPTH, 1, RW_WIDTH), rw_ln_b.reshape(DEPTH, 1, RW_WIDTH), w_out_bf,
               norm_ffn.reshape(DEPTH, 1, D_MODEL), rt_w, rt_b)

    x = jnp.concatenate([x_prompt.reshape(N_CTX_ROWS, D_MODEL), x_sample.reshape(N_LAT_ROWS, D_MODEL)], axis=0)
    keys, vals, states = [], [], []
    for l in range(DEPTH):
        conv, q, k, v, rw, cedge, redge = _inproj(x, mod3, norm_mix, w_in_bf, l)
        cpe, cne = _tile_edges(cedge)
        rpe, rne = _tile_edges(redge)

        sc_c, bonus_c = _rwkv_pre(rw, rpe, rne, rw_prm, l, 0, NT_CTX)
        sc_l, bonus_l = _rwkv_pre(rw, rpe, rne, rw_prm, l, NT_CTX, NT - NT_CTX)
        yf_l, yb_l, yf_c, yb_c, sfin = _rwkv_scan(sc_l.reshape(2, 6, DEC_BATCH, DEC_SEQ, RW_WIDTH),
                                                  sc_c.reshape(2, 6, BATCH, SEQ, RW_WIDTH), s0_lat[l],
                                                  DEC_BATCH, DEC_SEQ, BATCH, SEQ)
        yf = (yf_c.reshape(N_CTX_ROWS, RW_WIDTH), yf_l.reshape(N_LAT_ROWS, RW_WIDTH))
        yb = (yb_c.reshape(N_CTX_ROWS, RW_WIDTH), yb_l.reshape(N_LAT_ROWS, RW_WIDTH))
        bonus = (bonus_c, bonus_l)

        att_c = _ctx_attention(q, k, v, sink_b, l)
        att_l = _lat_attention(q, k, v, cache_k4, cache_v4, cos, sin, sink_b, l)
        att = (att_c, att_l)

        x_mid, h3, route, counts = _outproj(x, conv, cpe, cne, att, yf, yb, bonus, rw, mod3, out_prm, l)
        slots, block_exp, n_used, next_expert = _slot_plan(route, counts)
        xs = _dispatch(slots, h3, xs_zero)
        ys = _experts(block_exp, n_used, next_expert, xs, exp_w1, exp_w3, exp_w2, l)
        x = _combine(slots, x_mid, route, mod3, norm_out, ys, l, l == DEPTH - 1)

        keys.append(k[:N_CTX_ROWS].reshape(BATCH, SEQ, ATT_KV_HEADS, HEAD_DIM))
        vals.append(v[:N_CTX_ROWS].reshape(BATCH, SEQ, ATT_KV_HEADS, HEAD_DIM))
        states.append(_pairs_to_state(sfin))

    y_prompt = x[:N_CTX_ROWS].reshape(BATCH, SEQ, D_MODEL)
    y_sample = x[N_CTX_ROWS:].reshape(DEC_BATCH, DEC_SEQ, D_MODEL)
    return (y_prompt, y_sample, jnp.stack(keys, axis=1), jnp.stack(vals, axis=1), jnp.stack(states, axis=1))
```

```python
import functools

import jax
import jax.numpy as jnp
from jax import lax
from jax.experimental import pallas as pl
from jax.experimental.pallas import tpu as pltpu

F32 = jnp.float32
BF16 = jnp.bfloat16
HIGHEST = lax.Precision.HIGHEST

D_MODEL = 1024
BATCH = 16
SEQ = 256
DEPTH = 4
DEC_BATCH = 4
DEC_SEQ = 2048
PAST_LEN = 256
GRID_W = 64
HEAD_DIM = 64
CONV_CH = 256
ATT_WIDTH = 512
ATT_HEADS = 8
ATT_KV_HEADS = 2
ATT_GQ = 4
KV_WIDTH = 128
WINDOW = 128
ATT_BLOCK = 128
RW_WIDTH = 256
RW_HEADS = 4
LORA = 64
IN_WIDTH = 2816
N_GROUPS = 4
EXP_PER_GROUP = 8
N_EXPERTS = 32
TOP_K = 2
D_EXPERT = 512
ROPE_BASE = 10000.0
NORM_EPS = 1e-6
GN_EPS = 64e-5
DECAY_SCALE = 0.6065306597126334
NEG_INF = -1e30

LANES = 128
SUBLANES = 8
N_CTX_ROWS = BATCH * SEQ
N_LAT_ROWS = DEC_BATCH * DEC_SEQ
N_ROWS = N_CTX_ROWS + N_LAT_ROWS
TM = 256
NT = N_ROWS // TM
NT_CTX = N_CTX_ROWS // TM
TILES_PER_LAT = DEC_SEQ // TM
N_COND = 8
D_CHUNKS = D_MODEL // LANES
MOE_BLOCK = 256
N_ASG = N_ROWS * TOP_K
N_MOE_BLOCKS = N_ASG // MOE_BLOCK + N_EXPERTS
N_SLOTS = N_MOE_BLOCKS * MOE_BLOCK
VMEM_LIMIT = 56 * 1024 * 1024


def _params(sem, vmem=VMEM_LIMIT):
    return pltpu.CompilerParams(dimension_semantics=sem, vmem_limit_bytes=vmem)


def _cond_of_tile(i):
    return jnp.where(i < NT_CTX, 0, 1 + (i - NT_CTX) // TILES_PER_LAT)


def _mod_spec(layer, j):
    return pl.BlockSpec((1, 1, D_MODEL),
                        lambda i: ((layer * N_COND + _cond_of_tile(i)) * 6 + j, 0, 0))


def _layer_spec(shape, layer):
    nd = len(shape)
    return pl.BlockSpec((None,) + tuple(shape), lambda *_: (layer,) + (0,) * nd)


def _rms(x, g):
    return x * lax.rsqrt(jnp.mean(x * x, axis=-1, keepdims=True) + NORM_EPS) * g


def _bdot(a, b):
    return jnp.dot(a.astype(BF16), b.astype(BF16), preferred_element_type=F32)


def _mod_kernel(c_ref, w_ref, b_ref, o_ref):
    c = c_ref[...]
    s = c * jax.nn.sigmoid(c)
    o_ref[...] = jnp.dot(s, w_ref[...], preferred_element_type=F32, precision=HIGHEST) + b_ref[...]


def _modulation(cond, w_mod, b_mod):
    tn = 1536
    return pl.pallas_call(
        _mod_kernel,
        grid=(DEPTH, 6 * D_MODEL // tn),
        in_specs=[pl.BlockSpec((N_COND, D_MODEL), lambda l, j: (0, 0)),
                  pl.BlockSpec((None, D_MODEL, tn), lambda l, j: (l, 0, j)),
                  pl.BlockSpec((None, 1, tn), lambda l, j: (l, 0, j))],
        out_specs=pl.BlockSpec((None, N_COND, tn), lambda l, j: (l, 0, j)),
        out_shape=jax.ShapeDtypeStruct((DEPTH, N_COND, 6 * D_MODEL), F32),
        compiler_params=_params(("arbitrary", "arbitrary")),
        name="modulation",
    )(cond, w_mod, b_mod.reshape(DEPTH, 1, 6 * D_MODEL))


def _inproj_kernel(x_ref, sh_ref, sc_ref, g_ref, w_ref, conv_ref, q_ref, k_ref, v_ref, rw_ref, cedge_ref, redge_ref):
    h = _rms(x_ref[...], g_ref[...]) * (1.0 + sc_ref[0]) + sh_ref[0]
    p = jnp.dot(h.astype(BF16), w_ref[...], preferred_element_type=F32)
    conv_ref[...] = p[:, 0:768]
    q_ref[...] = p[:, 768:1280]
    k_ref[...] = p[:, 1280:1408]
    v_ref[...] = p[:, 1408:1536]
    rw_ref[...] = p[:, 1536:2816]
    cedge_ref[0, 0:1, :] = p[0:1, 0:768]
    cedge_ref[0, 1:2, :] = p[TM - 1:TM, 0:768]
    redge_ref[0, 0:1, :] = p[0:1, 1536:2560]
    redge_ref[0, 1:2, :] = p[TM - 1:TM, 1536:2560]


def _inproj(x, mod3, norm_mix, w_in_bf, layer):
    widths = (768, ATT_WIDTH, KV_WIDTH, KV_WIDTH, 5 * RW_WIDTH)
    edges = (768, 4 * RW_WIDTH)
    return pl.pallas_call(
        _inproj_kernel,
        grid=(NT,),
        in_specs=[pl.BlockSpec((TM, D_MODEL), lambda i: (i, 0)),
                  _mod_spec(layer, 0), _mod_spec(layer, 1),
                  _layer_spec((1, D_MODEL), layer),
                  _layer_spec((D_MODEL, IN_WIDTH), layer)],
        out_specs=[pl.BlockSpec((TM, w), lambda i: (i, 0)) for w in widths]
        + [pl.BlockSpec((1, 2, w), lambda i: (i, 0, 0)) for w in edges],
        out_shape=[jax.ShapeDtypeStruct((N_ROWS, w), F32) for w in widths]
        + [jax.ShapeDtypeStruct((NT, 2, w), F32) for w in edges],
        compiler_params=_params(("parallel",)),
        name="inproj",
    )(x, mod3, mod3, norm_mix.reshape(DEPTH, 1, D_MODEL), w_in_bf)


def _tile_edges(edge):
    c = edge.shape[-1]
    first, last = edge[:, 0], edge[:, 1]
    t = jnp.arange(NT)
    lat = t >= NT_CTX
    pos = (t - NT_CTX) % TILES_PER_LAT
    has_prev = (lat & (pos != 0))[:, None]
    has_next = (lat & (pos != TILES_PER_LAT - 1))[:, None]
    prev = jnp.where(has_prev, jnp.roll(last, 1, axis=0), 0.0)
    nxt = jnp.where(has_next, jnp.roll(first, -1, axis=0), 0.0)
    return prev.reshape(NT, 1, c), nxt.reshape(NT, 1, c)


def _shift_rows(x, prev_row, next_row):
    n = x.shape[0]
    row = lax.broadcasted_iota(jnp.int32, x.shape, 0)
    before = jnp.where(row == 0, prev_row, pltpu.roll(x, 1, axis=0))
    after = jnp.where(row == n - 1, next_row, pltpu.roll(x, n - 1, axis=0))
    return before, after


def _split_bf16(x):
    hi = x.astype(BF16)
    return hi, (x - hi.astype(F32)).astype(BF16)


def _head_ones(n):
    r = lax.broadcasted_iota(jnp.int32, (2 * n, n), 0) % n // HEAD_DIM
    c = lax.broadcasted_iota(jnp.int32, (2 * n, n), 1) // HEAD_DIM
    return (r == c).astype(BF16)


def _head_sum(x, ones):
    return jnp.dot(jnp.concatenate(_split_bf16(x), axis=1), ones, preferred_element_type=F32)


def _rwpre_kernel(x_ref, pe_ref, ne_ref, mu_ref, w0_ref, w1_ref, w2_ref, a0_ref, a1_ref, a2_ref,
                  kk_ref, ka_ref, rk_ref, sc_ref, bonus_ref):
    x = x_ref[...]
    before, after = _shift_rows(x, pe_ref[0], ne_ref[0])
    ones = _head_ones(RW_WIDTH)
    bonus = jnp.zeros((TM, RW_WIDTH), F32)
    for d in range(2):
        nbr = before if d == 0 else after
        xd = x + (nbr - x) * mu_ref[d:d + 1, :]
        r = xd[:, 0:256]
        k = xd[:, 256:512]
        v = xd[:, 512:768]
        z = xd[:, 768:1024]
        lw = _bdot(jnp.tanh(_bdot(z, w1_ref[d])), w2_ref[d])
        w = jnp.exp(-DECAY_SCALE * jax.nn.sigmoid(w0_ref[d:d + 1, :] + lw))
        la = _bdot(_bdot(z, a1_ref[d]), a2_ref[d])
        a = jax.nn.sigmoid(a0_ref[d:d + 1, :] + la)
        kk = k * kk_ref[d:d + 1, :]
        kk = kk / jnp.maximum(jnp.sqrt(_head_sum(kk * kk, ones)), 1e-12)
        k2 = k * (1.0 + (a - 1.0) * ka_ref[d:d + 1, :])
        bonus = bonus + _head_sum(r * k2 * rk_ref[d:d + 1, :], ones) * v
        sc_ref[d, 0] = r
        sc_ref[d, 1] = w
        sc_ref[d, 2] = k2
        sc_ref[d, 3] = v
        sc_ref[d, 4] = -kk
        sc_ref[d, 5] = kk * a
    bonus_ref[...] = bonus


def _rwkv_pre(rw, prev_e, next_e, prm, layer, tile0, ntiles):
    (mu, w0, w1, w2, a0, a1, a2, kk, ka, rk) = prm
    nrows = ntiles * TM
    return pl.pallas_call(
        _rwpre_kernel,
        grid=(ntiles,),
        in_specs=[pl.BlockSpec((TM, 4 * RW_WIDTH), lambda i: (i + tile0, 0)),
                  pl.BlockSpec((1, 1, 4 * RW_WIDTH), lambda i: (i + tile0, 0, 0)),
                  pl.BlockSpec((1, 1, 4 * RW_WIDTH), lambda i: (i + tile0, 0, 0)),
                  _layer_spec((2, 4 * RW_WIDTH), layer),
                  _layer_spec((2, RW_WIDTH), layer),
                  _layer_spec((2, RW_WIDTH, LORA), layer),
                  _layer_spec((2, LORA, RW_WIDTH), layer),
                  _layer_spec((2, RW_WIDTH), layer),
                  _layer_spec((2, RW_WIDTH, LORA), layer),
                  _layer_spec((2, LORA, RW_WIDTH), layer),
                  _layer_spec((2, RW_WIDTH), layer),
                  _layer_spec((2, RW_WIDTH), layer),
                  _layer_spec((2, RW_WIDTH), layer)],
        out_specs=[pl.BlockSpec((2, 6, TM, RW_WIDTH), lambda i: (0, 0, i, 0)),
                   pl.BlockSpec((TM, RW_WIDTH), lambda i: (i, 0))],
        out_shape=[jax.ShapeDtypeStruct((2, 6, nrows, RW_WIDTH), F32),
                   jax.ShapeDtypeStruct((nrows, RW_WIDTH), F32)],
        compiler_params=_params(("parallel",)),
        name="rwkv_pre",
    )(rw, prev_e, next_e, mu, w0, w1, w2, a0, a1, a2, kk, ka, rk)


def _half_ones():
    r = lax.broadcasted_iota(jnp.int32, (LANES, LANES), 0) // HEAD_DIM
    c = lax.broadcasted_iota(jnp.int32, (LANES, LANES), 1) // HEAD_DIM
    return (r == c).astype(BF16)


def _scan_kernel(lf_ref, lb_ref, c_ref, s0_ref, ylf_ref, ylb_ref, ycf_ref, ycb_ref, sfin_ref, stl_ref, stc_ref,
                 *, n_lat, n_ctx, tblk):
    tb = pl.program_id(0)

    @pl.when(tb == 0)
    def _():
        stl_ref[...] = s0_ref[...]

    stc_ref[...] = jnp.zeros_like(stc_ref)

    ones = _half_ones()
    vi = lax.broadcasted_iota(jnp.int32, (HEAD_DIM, LANES), 0)
    li = lax.broadcasted_iota(jnp.int32, (HEAD_DIM, LANES), 1)
    diag = vi == (li % HEAD_DIM)
    diag_bf = diag.astype(BF16)
    r_row, w_row, k_row, v_row, a_row, b_row = range(6)
    seqs = [("lat", b, d) for b in range(n_lat) for d in range(2)] + \
           [("ctx", b, d) for b in range(n_ctx) for d in range(2)]
    chains = [(q, p) for q in seqs for p in range(2)]

    def row_of(t, d):
        return t if d == 0 else tblk - 1 - t

    def in_row(q, j, t):
        kind, b, d = q
        if kind == "lat":
            return (lf_ref if d == 0 else lb_ref)[j, b, pl.ds(row_of(t, d), 1), :]
        return c_ref[d, j, b, pl.ds(row_of(t, d), 1), :]

    def y_ref_of(q):
        kind, _, d = q
        if kind == "lat":
            return ylf_ref if d == 0 else ylb_ref
        return ycf_ref if d == 0 else ycb_ref

    def state_ref(q):
        return stl_ref if q[0] == "lat" else stc_ref

    def load_rows(t, which):
        rows = {}
        for q in seqs:
            full = {j: in_row(q, j, t) for j in which}
            for p in range(2):
                rows[(q, p)] = {j: x[:, p * LANES:(p + 1) * LANES] for j, x in full.items()}
        return rows

    def states():
        return {(q, p): state_ref(q)[q[1], q[2], p] for (q, p) in chains}

    def emit_y(t, ss, rows):
        pr = jnp.concatenate([(ss[c] * rows[c][r_row]).astype(BF16) for c in chains], axis=0)
        ybc = jnp.dot(pr, ones, preferred_element_type=F32)
        ys = {}
        for i, c in enumerate(chains):
            blk = ybc[i * HEAD_DIM:(i + 1) * HEAD_DIM]
            ys[c] = jnp.sum(jnp.where(diag, blk, 0.0), axis=0, keepdims=True)
        for q in seqs:
            y_ref_of(q)[q[1], pl.ds(row_of(t, q[2]), 1), :] = jnp.concatenate([ys[(q, 0)], ys[(q, 1)]], axis=1)

    def step(t, carry):
        rows = load_rows(t, (w_row, k_row, v_row, a_row, b_row))
        tp = jnp.maximum(t - 1, 0)
        ss = states()
        emit_y(tp, ss, load_rows(tp, (r_row,)))
        pa = jnp.concatenate([(ss[c] * rows[c][a_row]).astype(BF16) for c in chains], axis=0)
        dv = jnp.concatenate([diag_bf * rows[c][v_row].astype(BF16) for c in chains], axis=0)
        sa = jnp.dot(pa, ones, preferred_element_type=F32)
        vc = jnp.dot(dv, ones, preferred_element_type=F32)
        for i, c in enumerate(chains):
            (q, p), rw = c, rows[c]
            sl = slice(i * HEAD_DIM, (i + 1) * HEAD_DIM)
            state_ref(q)[q[1], q[2], p] = ss[c] * rw[w_row] + sa[sl] * rw[b_row] + vc[sl] * rw[k_row]
        return carry

    lax.fori_loop(0, tblk, step, 0)
    emit_y(tblk - 1, states(), load_rows(tblk - 1, (r_row,)))
    sfin_ref[...] = stc_ref[...]


def _rwkv_scan(sc_lat, sc_ctx, s0_lat, n_lat, lat_len, n_ctx_total, tblk):
    ntb = lat_len // tblk
    n_ctx = n_ctx_total // ntb
    kern = functools.partial(_scan_kernel, n_lat=n_lat, n_ctx=n_ctx, tblk=tblk)
    st_shape = (2, 2, HEAD_DIM, LANES)
    return pl.pallas_call(
        kern,
        grid=(ntb,),
        in_specs=[pl.BlockSpec((None, 6, n_lat, tblk, RW_WIDTH), lambda t: (0, 0, 0, t, 0)),
                  pl.BlockSpec((None, 6, n_lat, tblk, RW_WIDTH), lambda t: (1, 0, 0, ntb - 1 - t, 0)),
                  pl.BlockSpec((2, 6, n_ctx, tblk, RW_WIDTH), lambda t: (0, 0, t, 0, 0)),
                  pl.BlockSpec((n_lat,) + st_shape, lambda t: (0, 0, 0, 0, 0))],
        out_specs=[pl.BlockSpec((n_lat, tblk, RW_WIDTH), lambda t: (0, t, 0)),
                   pl.BlockSpec((n_lat, tblk, RW_WIDTH), lambda t: (0, ntb - 1 - t, 0)),
                   pl.BlockSpec((n_ctx, tblk, RW_WIDTH), lambda t: (t, 0, 0)),
                   pl.BlockSpec((n_ctx, tblk, RW_WIDTH), lambda t: (t, 0, 0)),
                   pl.BlockSpec((n_ctx,) + st_shape, lambda t: (t, 0, 0, 0, 0))],
        out_shape=[jax.ShapeDtypeStruct((n_lat, lat_len, RW_WIDTH), F32),
                   jax.ShapeDtypeStruct((n_lat, lat_len, RW_WIDTH), F32),
                   jax.ShapeDtypeStruct((n_ctx_total, tblk, RW_WIDTH), F32),
                   jax.ShapeDtypeStruct((n_ctx_total, tblk, RW_WIDTH), F32),
                   jax.ShapeDtypeStruct((n_ctx_total,) + st_shape, F32)],
        scratch_shapes=[pltpu.VMEM((n_lat,) + st_shape, F32), pltpu.VMEM((n_ctx,) + st_shape, F32)],
        compiler_params=_params(("arbitrary",)),
        name="rwkv_scan",
    )(sc_lat, sc_lat, sc_ctx, s0_lat)


def _state_to_pairs(s):
    b = s.shape[0]
    return s.reshape(b, 2, 2, 2, HEAD_DIM, HEAD_DIM).transpose(0, 1, 2, 4, 3, 5).reshape(b, 2, 2, HEAD_DIM, LANES)


def _pairs_to_state(s):
    b = s.shape[0]
    return s.reshape(b, 2, 2, HEAD_DIM, 2, HEAD_DIM).transpose(0, 1, 2, 4, 3, 5).reshape(b, 2, RW_HEADS, HEAD_DIM, HEAD_DIM)


def _rope(x, cos, sin):
    n = x.shape[1]
    m = n // LANES
    if m > 1:
        cos = jnp.concatenate([cos] * m, axis=1)
        sin = jnp.concatenate([sin] * m, axis=1)
    lane = lax.broadcasted_iota(jnp.int32, x.shape, 1)
    swapped = jnp.where((lane % 32) < 16, pltpu.roll(x, n - 16, axis=1), pltpu.roll(x, 16, axis=1))
    return x * cos + swapped * sin


def _attend(q, keys, vals, biases, sink_ref, o_ref):
    nq = q.shape[0]
    qb = (q * (HEAD_DIM ** -0.5)).astype(BF16)
    kbs = [k.astype(BF16) for k in keys]
    vbs = [v.astype(BF16) for v in vals]
    tiled = [None if b is None else jnp.concatenate([b] * ATT_GQ, axis=0) for b in biases]
    for hk in range(ATT_KV_HEADS):
        heads = range(hk * ATT_GQ, (hk + 1) * ATT_GQ)
        kv = slice(hk * HEAD_DIM, (hk + 1) * HEAD_DIM)
        qg = jnp.concatenate([qb[:, h * HEAD_DIM:(h + 1) * HEAD_DIM] for h in heads], axis=0)
        sink = jnp.concatenate([jnp.broadcast_to(sink_ref[h:h + 1, 0:1], (nq, 1)) for h in heads], axis=0)
        ss = []
        for kb, bias in zip(kbs, tiled):
            s = lax.dot_general(qg, kb[:, kv], (((1,), (1,)), ((), ())), preferred_element_type=F32)
            ss.append(s if bias is None else s + bias)
        m_lanes = ss[0]
        for s in ss[1:]:
            m_lanes = jnp.maximum(m_lanes, s)
        m = jnp.maximum(jnp.max(m_lanes, axis=-1, keepdims=True), sink)
        l_lanes = None
        acc = jnp.zeros((ATT_GQ * nq, HEAD_DIM), F32)
        for s, vb in zip(ss, vbs):
            p = jnp.exp(s - m)
            l_lanes = p if l_lanes is None else l_lanes + p
            acc = acc + jnp.dot(p.astype(BF16), vb[:, kv], preferred_element_type=F32)
        out = acc / (jnp.sum(l_lanes, axis=-1, keepdims=True) + jnp.exp(sink - m))
        for g, h in enumerate(heads):
            o_ref[:, h * HEAD_DIM:(h + 1) * HEAD_DIM] = out[g * nq:(g + 1) * nq]


def _ctx_attn_kernel(q_ref, k_ref, v_ref, sink_ref, o_ref):
    chunks = range(0, SEQ, ATT_BLOCK)
    _attend(q_ref[...], [k_ref[c:c + ATT_BLOCK, :] for c in chunks], [v_ref[c:c + ATT_BLOCK, :] for c in chunks],
            [None for _ in chunks], sink_ref, o_ref)


def _ctx_attention(q, k, v, sink_b, layer):
    nq = SEQ // ATT_BLOCK
    return pl.pallas_call(
        _ctx_attn_kernel,
        grid=(BATCH, nq),
        in_specs=[pl.BlockSpec((ATT_BLOCK, ATT_WIDTH), lambda b, i: (b * nq + i, 0)),
                  pl.BlockSpec((SEQ, KV_WIDTH), lambda b, i: (b, 0)),
                  pl.BlockSpec((SEQ, KV_WIDTH), lambda b, i: (b, 0)),
                  pl.BlockSpec((None, ATT_HEADS, LANES), lambda b, i: (layer, 0, 0))],
        out_specs=pl.BlockSpec((ATT_BLOCK, ATT_WIDTH), lambda b, i: (b * nq + i, 0)),
        out_shape=jax.ShapeDtypeStruct((N_CTX_ROWS, ATT_WIDTH), F32),
        compiler_params=_params(("parallel", "parallel")),
        name="ctx_attention",
    )(q, k, v, sink_b)


def _lat_attn_kernel(q_ref, k_ref, v_ref, ck_ref, cv_ref, cos_ref, sin_ref, sink_ref, o_ref):
    n = pl.program_id(1)
    nb = pl.num_programs(1)
    r0 = pl.multiple_of(n * ATT_BLOCK, ATT_BLOCK)
    q = _rope(q_ref[...], cos_ref[pl.ds(r0, ATT_BLOCK), :], sin_ref[pl.ds(r0, ATT_BLOCK), :])
    qi = lax.broadcasted_iota(jnp.int32, (ATT_BLOCK, ATT_BLOCK), 0)
    ki = lax.broadcasted_iota(jnp.int32, (ATT_BLOCK, ATT_BLOCK), 1)
    keys, vals, masks = [], [], []
    for j in (-1, 0, 1):
        blk = jnp.clip(n + j, 0, nb - 1)
        k0 = pl.multiple_of(blk * ATT_BLOCK, ATT_BLOCK)
        kb = _rope(k_ref[pl.ds(k0, ATT_BLOCK), :], cos_ref[pl.ds(k0, ATT_BLOCK), :],
                   sin_ref[pl.ds(k0, ATT_BLOCK), :])
        keys.append(kb)
        vals.append(v_ref[pl.ds(k0, ATT_BLOCK), :])
        rel = ki + j * ATT_BLOCK - qi
        reach = jnp.where((n + j >= 0) & (n + j < nb), WINDOW, -1)
        masks.append(jnp.where(jnp.abs(rel) <= reach, 0.0, NEG_INF))
    for c in range(0, PAST_LEN, ATT_BLOCK):
        keys.append(ck_ref[c:c + ATT_BLOCK, :])
        vals.append(cv_ref[c:c + ATT_BLOCK, :])
        masks.append(None)
    _attend(q, keys, vals, masks, sink_ref, o_ref)


def _lat_attention(q, k, v, cache_k4, cache_v4, cos, sin, sink_b, layer):
    nb = DEC_SEQ // ATT_BLOCK
    qoff = N_CTX_ROWS // ATT_BLOCK
    soff = N_CTX_ROWS // DEC_SEQ
    return pl.pallas_call(
        _lat_attn_kernel,
        grid=(DEC_BATCH, nb),
        in_specs=[pl.BlockSpec((ATT_BLOCK, ATT_WIDTH), lambda b, i: (qoff + b * nb + i, 0)),
                  pl.BlockSpec((DEC_SEQ, KV_WIDTH), lambda b, i: (soff + b, 0)),
                  pl.BlockSpec((DEC_SEQ, KV_WIDTH), lambda b, i: (soff + b, 0)),
                  pl.BlockSpec((None, None, PAST_LEN, KV_WIDTH), lambda b, i: (b, layer, 0, 0)),
                  pl.BlockSpec((None, None, PAST_LEN, KV_WIDTH), lambda b, i: (b, layer, 0, 0)),
                  pl.BlockSpec((DEC_SEQ, LANES), lambda b, i: (0, 0)),
                  pl.BlockSpec((DEC_SEQ, LANES), lambda b, i: (0, 0)),
                  pl.BlockSpec((None, ATT_HEADS, LANES), lambda b, i: (layer, 0, 0))],
        out_specs=pl.BlockSpec((ATT_BLOCK, ATT_WIDTH), lambda b, i: (b * nb + i, 0)),
        out_shape=jax.ShapeDtypeStruct((N_LAT_ROWS, ATT_WIDTH), F32),
        compiler_params=_params(("parallel", "parallel")),
        name="lat_attention",
    )(q, k, v, cache_k4, cache_v4, cos, sin, sink_b)


def _rope_tables():
    pos = jnp.arange(DEC_SEQ)
    rows = (pos // GRID_W).astype(F32)
    cols = (pos % GRID_W).astype(F32)
    inv = ROPE_BASE ** (-jnp.arange(0, 32, 2, dtype=F32) / 32)
    lane = jnp.arange(LANES)
    posl = jnp.where(((lane % HEAD_DIM) // 32 == 0)[None, :], rows[:, None], cols[:, None])
    ang = posl * inv[lane % 16][None, :]
    sign = jnp.where((lane % 32) < 16, -1.0, 1.0)[None, :]
    return jnp.cos(ang), jnp.sin(ang) * sign


def _outproj_kernel(x_ref, conv_ref, cpe_ref, cne_ref, attc_ref, attl_ref, yfc_ref, yfl_ref, ybc_ref, ybl_ref,
                    bonc_ref, bonl_ref, g_ref,
                    cw_ref, lng_ref, lnb_ref, wout_ref, gate_ref, sh_ref, sc_ref, nf_ref, rtw_ref, rtb_ref,
                    xo_ref, h3_ref, route_ref, counts_ref, cnt_ref):
    is_ctx = pl.program_id(0) < NT_CTX
    att = jnp.where(is_ctx, attc_ref[...], attl_ref[...])
    ysum = jnp.where(is_ctx, yfc_ref[...] + ybc_ref[...], yfl_ref[...] + ybl_ref[...])
    bonus = jnp.where(is_ctx, bonc_ref[...], bonl_ref[...])
    conv = conv_ref[...]
    u = conv[:, 256:512] * conv[:, 512:768]
    pe = cpe_ref[0]
    ne = cne_ref[0]
    before, after = _shift_rows(u, pe[:, 256:512] * pe[:, 512:768], ne[:, 256:512] * ne[:, 512:768])
    y_conv = conv[:, 0:256] * (before * cw_ref[0:1, :] + u * cw_ref[1:2, :] + after * cw_ref[2:3, :])

    ones = _head_ones(RW_WIDTH)
    inv_hd = 1.0 / HEAD_DIM
    mean = _head_sum(ysum, ones) * inv_hd
    cen = ysum - mean
    var = _head_sum(cen * cen, ones) * inv_hd
    yn = cen * lax.rsqrt(var + GN_EPS) * lng_ref[...] + lnb_ref[...]
    y_rw = (yn + bonus) * jax.nn.sigmoid(g_ref[...])

    mix = (jnp.dot(y_conv.astype(BF16), wout_ref[0:256, :], preferred_element_type=F32)
           + jnp.dot(att.astype(BF16), wout_ref[256:768, :], preferred_element_type=F32)
           + jnp.dot(y_rw.astype(BF16), wout_ref[768:1024, :], preferred_element_type=F32))
    x = x_ref[...] + gate_ref[0] * mix
    xo_ref[...] = x
    h2 = _rms(x, nf_ref[...]) * (1.0 + sc_ref[0]) + sh_ref[0]
    for j in range(D_CHUNKS):
        h3_ref[pl.ds(j, TM, stride=D_CHUNKS), :] = h2[:, j * LANES:(j + 1) * LANES]

    h_hi, h_lo = _split_bf16(h2)
    logits = jnp.dot(jnp.concatenate([h_hi, h_lo, h_hi], axis=1), rtw_ref[...],
                     preferred_element_type=F32) + rtb_ref[...]
    lane = lax.broadcasted_iota(jnp.int32, logits.shape, 1)
    big = jnp.int32(1 << 20)
    is_g = lane < N_GROUPS
    gmax = jnp.max(jnp.where(is_g, logits, NEG_INF), axis=-1, keepdims=True)
    gsum = jnp.sum(jnp.where(is_g, jnp.exp(logits - gmax), 0.0), axis=-1, keepdims=True)
    g_top = 1.0 / gsum
    g_idx = jnp.min(jnp.where(is_g & (logits == gmax), lane, big), axis=-1, keepdims=True)
    lo = N_GROUPS + g_idx * EXP_PER_GROUP
    in_grp = (lane >= lo) & (lane < lo + EXP_PER_GROUP)
    emax = jnp.max(jnp.where(in_grp, logits, NEG_INF), axis=-1, keepdims=True)
    pe_ = jnp.where(in_grp, jnp.exp(logits - emax), -1.0)
    p1 = jnp.max(pe_, axis=-1, keepdims=True)
    l1 = jnp.min(jnp.where(pe_ == p1, lane, big), axis=-1, keepdims=True)
    pe2 = jnp.where(lane == l1, -1.0, pe_)
    p2 = jnp.max(pe2, axis=-1, keepdims=True)
    l2 = jnp.min(jnp.where(pe2 == p2, lane, big), axis=-1, keepdims=True)
    wsum = p1 + p2
    w1 = g_top * p1 / wsum
    w2 = g_top * p2 / wsum
    e1 = (l1 - N_GROUPS).astype(F32)
    e2 = (l2 - N_GROUPS).astype(F32)

    @pl.when(pl.program_id(0) == 0)
    def _():
        cnt_ref[...] = jnp.zeros_like(cnt_ref)

    hit1 = lane == l1 - N_GROUPS
    hit2 = lane == l2 - N_GROUPS
    onehot = jnp.where(hit1 | hit2, 1.0, 0.0)
    ri = lax.broadcasted_iota(jnp.int32, (TM, TM), 0)
    ci = lax.broadcasted_iota(jnp.int32, (TM, TM), 1)
    earlier = jnp.where(ci < ri, 1.0, 0.0).astype(BF16)
    before = jnp.dot(earlier, onehot.astype(BF16), preferred_element_type=F32) + cnt_ref[...]
    r1 = jnp.sum(jnp.where(hit1, before, 0.0), axis=-1, keepdims=True)
    r2 = jnp.sum(jnp.where(hit2, before, 0.0), axis=-1, keepdims=True)
    cnt_ref[...] = cnt_ref[...] + jnp.sum(onehot, axis=0, keepdims=True)
    counts_ref[...] = jnp.broadcast_to(cnt_ref[...], counts_ref.shape)

    route = jnp.zeros_like(logits)
    for j, col in enumerate((e1, e2, w1, w2, r1, r2)):
        route = jnp.where(lane == j, col, route)
    route_ref[...] = route


def _outproj(x, conv, cpe, cne, att, yf, yb, bonus, rw, mod3, prm, layer):
    conv_w, ln_g, ln_b, w_out_bf, norm_ffn, rt_w, rt_b = prm
    row = lambda w: pl.BlockSpec((TM, w), lambda i: (i, 0))
    ctx = lambda w: pl.BlockSpec((TM, w), lambda i: (jnp.minimum(i, NT_CTX - 1), 0))
    lat = lambda w: pl.BlockSpec((TM, w), lambda i: (jnp.maximum(i - NT_CTX, 0), 0))
    edge = pl.BlockSpec((1, 1, 768), lambda i: (i, 0, 0))
    return pl.pallas_call(
        _outproj_kernel,
        grid=(NT,),
        in_specs=[row(D_MODEL), row(768), edge, edge, ctx(ATT_WIDTH), lat(ATT_WIDTH), ctx(RW_WIDTH), lat(RW_WIDTH),
                  ctx(RW_WIDTH), lat(RW_WIDTH), ctx(RW_WIDTH), lat(RW_WIDTH),
                  pl.BlockSpec((TM, RW_WIDTH), lambda i: (i, 4)),
                  _layer_spec((3, CONV_CH), layer), _layer_spec((1, RW_WIDTH), layer),
                  _layer_spec((1, RW_WIDTH), layer), _layer_spec((D_MODEL, D_MODEL), layer),
                  _mod_spec(layer, 2), _mod_spec(layer, 3), _mod_spec(layer, 4),
                  _layer_spec((1, D_MODEL), layer), _layer_spec((3 * D_MODEL, LANES), layer),
                  _layer_spec((1, LANES), layer)],
        out_specs=[row(D_MODEL), pl.BlockSpec((TM * D_CHUNKS, LANES), lambda i: (i, 0)), row(LANES),
                   pl.BlockSpec((SUBLANES, LANES), lambda i: (0, 0))],
        out_shape=[jax.ShapeDtypeStruct((N_ROWS, D_MODEL), F32),
                   jax.ShapeDtypeStruct((N_ROWS * D_CHUNKS, LANES), F32),
                   jax.ShapeDtypeStruct((N_ROWS, LANES), F32),
                   jax.ShapeDtypeStruct((SUBLANES, LANES), F32)],
        scratch_shapes=[pltpu.VMEM((1, LANES), F32)],
        compiler_params=_params(("arbitrary",)),
        name="outproj_router",
    )(x, conv, cpe, cne, *att, *yf, *yb, *bonus, rw, conv_w, ln_g, ln_b, w_out_bf, mod3, mod3, mod3,
      norm_ffn, rt_w, rt_b)


def _slot_kernel(route_ref, start_ref, o_ref):
    route = route_ref[...]
    lane = lax.broadcasted_iota(jnp.int32, route.shape, 1).astype(F32)
    start = start_ref[...]
    cols = [route[:, 4 + k:5 + k] + jnp.sum(jnp.where(lane == route[:, k:k + 1], start, 0.0), axis=-1, keepdims=True)
            for k in range(TOP_K)]
    o_ref[...] = jnp.where(lane == 0.0, cols[0], jnp.where(lane == 1.0, cols[1], 0.0)).astype(jnp.int32)


def _slot_plan(route, counts):
    counts = counts[0, 0:N_EXPERTS].astype(jnp.int32)
    padded = (counts + MOE_BLOCK - 1) // MOE_BLOCK * MOE_BLOCK
    pad_end = jnp.cumsum(padded)
    pad_start = pad_end - padded
    start_row = jnp.zeros((1, LANES), F32).at[0, 0:N_EXPERTS].set(pad_start.astype(F32))
    slots = pl.pallas_call(
        _slot_kernel,
        grid=(N_ROWS // DEC_SEQ,),
        in_specs=[pl.BlockSpec((DEC_SEQ, LANES), lambda i: (i, 0)), pl.BlockSpec((1, LANES), lambda i: (0, 0))],
        out_specs=pl.BlockSpec((DEC_SEQ, LANES), lambda i: (i, 0)),
        out_shape=jax.ShapeDtypeStruct((N_ROWS, LANES), jnp.int32),
        compiler_params=_params(("parallel",)),
        name="moe_slots",
    )(route, start_row)[:, 0:TOP_K]
    blk0 = jnp.arange(N_MOE_BLOCKS, dtype=jnp.int32) * MOE_BLOCK
    block_exp = jnp.sum((pad_end[None, :] <= blk0[:, None]).astype(jnp.int32), axis=1)
    block_exp = jnp.minimum(block_exp, N_EXPERTS - 1)
    n_used = (pad_end[-1] // MOE_BLOCK).astype(jnp.int32)
    present = jnp.where(counts > 0, jnp.arange(N_EXPERTS, dtype=jnp.int32), N_EXPERTS)
    later = jnp.flip(lax.cummin(jnp.flip(present)))
    next_expert = jnp.concatenate([later[1:], jnp.full((1,), N_EXPERTS, jnp.int32)])
    return slots.reshape(-1), block_exp, n_used.reshape(1), next_expert


def _dispatch_kernel(slots_ref, h3_ref, xs_in_ref, xs_ref, sems):
    del xs_in_ref
    i = pl.program_id(0)
    last = pl.num_programs(0) - 1

    def row_copy(tok, s, sem):
        src = h3_ref.at[pl.ds(pl.multiple_of(tok * D_CHUNKS, D_CHUNKS), D_CHUNKS)]
        dst = xs_ref.at[pl.ds(pl.multiple_of(s * D_CHUNKS, D_CHUNKS), D_CHUNKS)]
        return pltpu.make_async_copy(src, dst, sem)

    def issue(r, c):
        tok = i * TM + r
        for k in range(TOP_K):
            row_copy(tok, slots_ref[tok * TOP_K + k], sems.at[i % 2]).start(priority=k)
        return c

    lax.fori_loop(0, TM, issue, 0)

    def drain(sem):
        def body(r, c):
            for k in range(TOP_K):
                row_copy(0, 0, sem).wait()
            return c
        lax.fori_loop(0, TM, body, 0)

    @pl.when(i > 0)
    def _():
        drain(sems.at[(i + 1) % 2])

    @pl.when(i == last)
    def _():
        drain(sems.at[i % 2])


def _dispatch(slots, h3, xs_zero):
    return pl.pallas_call(
        _dispatch_kernel,
        grid_spec=pltpu.PrefetchScalarGridSpec(
            num_scalar_prefetch=1,
            grid=(NT,),
            in_specs=[pl.BlockSpec(memory_space=pl.ANY),
                      pl.BlockSpec(memory_space=pl.ANY)],
            out_specs=pl.BlockSpec(memory_space=pl.ANY),
            scratch_shapes=[pltpu.SemaphoreType.DMA((2,))]),
        out_shape=jax.ShapeDtypeStruct((N_SLOTS * D_CHUNKS, LANES), F32),
        input_output_aliases={2: 0},
        compiler_params=_params(("arbitrary",)),
        name="moe_dispatch",
    )(slots, h3, xs_zero)


def _expert_kernel(be_ref, nu_ref, nxt_ref, xs_ref, w1_hbm, w3_hbm, w2_hbm, ys_ref,
                   w1f, w3f, w2f, w1b, w3b, w2b, sems, turn_ref, *, layer):
    i = pl.program_id(0)
    n_used = nu_ref[0]
    blk = jnp.minimum(i, n_used - 1)
    expert = be_ref[blk]
    prev_expert = be_ref[jnp.maximum(blk - 1, 0)]

    def fetch(e, slot):
        return [pltpu.make_async_copy(hbm.at[layer, e], buf.at[slot], sems.at[slot])
                for hbm, buf in ((w1_hbm, w1f), (w3_hbm, w3f), (w2_hbm, w2f))]

    @pl.when(i == 0)
    def _():
        turn_ref[0] = 0
        for cp in fetch(expert, 0):
            cp.start()

    @pl.when((i < n_used) & ((i == 0) | (expert != prev_expert)))
    def _():
        slot = turn_ref[0] % 2
        for cp in fetch(expert, slot):
            cp.wait()
        nxt = nxt_ref[expert]

        @pl.when(nxt < N_EXPERTS)
        def _():
            for cp in fetch(nxt, 1 - slot):
                cp.start()

        w1b[...] = w1f[slot].astype(BF16)
        w3b[...] = w3f[slot].astype(BF16)
        w2b[...] = w2f[slot].astype(BF16)
        turn_ref[0] = turn_ref[0] + 1

    @pl.when(i < n_used)
    def _():
        x = jnp.concatenate([xs_ref[pl.ds(j, MOE_BLOCK, stride=D_CHUNKS), :].astype(BF16)
                             for j in range(D_CHUNKS)], axis=1)
        a = jnp.dot(x, w1b[...], preferred_element_type=F32)
        b = jnp.dot(x, w3b[...], preferred_element_type=F32)
        hid = (a * jax.nn.sigmoid(a) * b).astype(BF16)
        y = jnp.dot(hid, w2b[...], preferred_element_type=F32)
        for j in range(D_CHUNKS):
            ys_ref[pl.ds(j, MOE_BLOCK, stride=D_CHUNKS), :] = y[:, j * LANES:(j + 1) * LANES]

    @pl.when(i >= n_used)
    def _():
        ys_ref[...] = jnp.zeros_like(ys_ref)


def _experts(block_exp, n_used, next_expert, xs, w1, w3, w2, layer):
    rows = pl.BlockSpec((MOE_BLOCK * D_CHUNKS, LANES), lambda i, be, nu, nx: (i, 0))
    hbm = pl.BlockSpec(memory_space=pl.ANY)
    return pl.pallas_call(
        functools.partial(_expert_kernel, layer=layer),
        grid_spec=pltpu.PrefetchScalarGridSpec(
            num_scalar_prefetch=3,
            grid=(N_MOE_BLOCKS,),
            in_specs=[rows, hbm, hbm, hbm],
            out_specs=rows,
            scratch_shapes=[pltpu.VMEM((2, D_MODEL, D_EXPERT), F32), pltpu.VMEM((2, D_MODEL, D_EXPERT), F32),
                            pltpu.VMEM((2, D_EXPERT, D_MODEL), F32),
                            pltpu.VMEM((D_MODEL, D_EXPERT), BF16), pltpu.VMEM((D_MODEL, D_EXPERT), BF16),
                            pltpu.VMEM((D_EXPERT, D_MODEL), BF16),
                            pltpu.SemaphoreType.DMA((2,)), pltpu.SMEM((1,), jnp.int32)]),
        out_shape=jax.ShapeDtypeStruct((N_SLOTS * D_CHUNKS, LANES), F32),
        compiler_params=_params(("arbitrary",)),
        name="moe_experts",
    )(block_exp, n_used, next_expert, xs, w1, w3, w2)


def _combine_kernel(slots_ref, x_ref, route_ref, gate_ref, g_ref, ys_ref, o_ref, buf, sems, *, final_norm):
    i = pl.program_id(0)
    last = pl.num_programs(0) - 1
    half = i % 2

    def row_copy(s, h, k, r):
        src = ys_ref.at[pl.ds(pl.multiple_of(s * D_CHUNKS, D_CHUNKS), D_CHUNKS)]
        dst = buf.at[h, k, pl.ds(pl.multiple_of(r * D_CHUNKS, D_CHUNKS), D_CHUNKS)]
        return pltpu.make_async_copy(src, dst, sems.at[h])

    def gather_tile(tile, h):
        def issue(r, c):
            for k in range(TOP_K):
                row_copy(slots_ref[(tile * TM + r) * TOP_K + k], h, k, r).start(priority=k)
            return c
        lax.fori_loop(0, TM, issue, 0)

    @pl.when(i == 0)
    def _():
        gather_tile(0, 0)

    @pl.when(i < last)
    def _():
        gather_tile(i + 1, 1 - half)

    def drain(r, c):
        for k in range(TOP_K):
            row_copy(0, half, k, 0).wait()
        return c

    lax.fori_loop(0, TM, drain, 0)
    route = route_ref[...]
    w1 = route[:, 2:3]
    w2 = route[:, 3:4]
    y = jnp.concatenate([w1 * buf[half, 0, pl.ds(j, TM, stride=D_CHUNKS), :]
                         + w2 * buf[half, 1, pl.ds(j, TM, stride=D_CHUNKS), :]
                         for j in range(D_CHUNKS)], axis=1)
    x = x_ref[...] + gate_ref[0] * y
    if final_norm:
        x = _rms(x, g_ref[...])
    o_ref[...] = x


def _combine(slots, x, route, mod3, norm_out, ys, layer, final_norm):
    kern = functools.partial(_combine_kernel, final_norm=final_norm)
    return pl.pallas_call(
        kern,
        grid_spec=pltpu.PrefetchScalarGridSpec(
            num_scalar_prefetch=1,
            grid=(NT,),
            in_specs=[pl.BlockSpec((TM, D_MODEL), lambda i, s: (i, 0)),
                      pl.BlockSpec((TM, LANES), lambda i, s: (i, 0)),
                      pl.BlockSpec((1, 1, D_MODEL),
                                   lambda i, s: ((layer * N_COND + _cond_of_tile(i)) * 6 + 5, 0, 0)),
                      pl.BlockSpec((1, D_MODEL), lambda i, s: (0, 0)),
                      pl.BlockSpec(memory_space=pl.ANY)],
            out_specs=pl.BlockSpec((TM, D_MODEL), lambda i, s: (i, 0)),
            scratch_shapes=[pltpu.VMEM((2, TOP_K, TM * D_CHUNKS, LANES), F32),
                            pltpu.SemaphoreType.DMA((2,))]),
        out_shape=jax.ShapeDtypeStruct((N_ROWS, D_MODEL), F32),
        compiler_params=_params(("arbitrary",)),
        name="moe_combine",
    )(slots, x, route, mod3, norm_out.reshape(1, D_MODEL), ys)


def kernel(x_prompt, x_sample, cache_k, cache_v, state_wkv, c, c_ctx, w_mod, b_mod, norm_mix, norm_ffn,
           norm_out, w_in, w_out, conv_w, attn_sink, rw_mu, rw_w0, rw_w1, rw_w2, rw_a0, rw_a1, rw_a2,
           rw_kk, rw_ka, rw_rk, rw_ln_g, rw_ln_b, rt_group_w, rt_group_b, rt_exp_w, rt_exp_b,
           exp_w1, exp_w3, exp_w2):
    cond = jnp.concatenate([c_ctx[None, :], c, jnp.zeros((N_COND - 1 - DEC_BATCH, D_MODEL), F32)], axis=0)
    mod3 = _modulation(cond, w_mod, b_mod).reshape(DEPTH * N_COND * 6, 1, D_MODEL)

    w_in_bf = w_in.astype(BF16)
    w_out_bf = w_out.astype(BF16)
    pad = LANES - N_GROUPS - N_EXPERTS
    rt_w = jnp.concatenate([rt_group_w, rt_exp_w, jnp.zeros((DEPTH, D_MODEL, pad), F32)], axis=-1)
    rt_hi, rt_lo = _split_bf16(rt_w)
    rt_w = jnp.concatenate([rt_hi, rt_hi, rt_lo], axis=1)
    rt_b = jnp.concatenate([rt_group_b, rt_exp_b, jnp.zeros((DEPTH, pad), F32)], axis=-1).reshape(DEPTH, 1, LANES)
    sink_b = jnp.broadcast_to(attn_sink[:, :, None], (DEPTH, ATT_HEADS, LANES))
    cos, sin = _rope_tables()
    cache_k4 = cache_k.reshape(DEC_BATCH, DEPTH, PAST_LEN, KV_WIDTH)
    cache_v4 = cache_v.reshape(DEC_BATCH, DEPTH, PAST_LEN, KV_WIDTH)
    s0_lat = _state_to_pairs(state_wkv.transpose(1, 0, 2, 3, 4, 5).reshape(DEPTH * DEC_BATCH, 2, RW_HEADS, HEAD_DIM, HEAD_DIM))
    s0_lat = s0_lat.reshape(DEPTH, DEC_BATCH, 2, 2, HEAD_DIM, LANES)
    xs_zero = jnp.zeros((N_SLOTS * D_CHUNKS, LANES), F32)
    rw_prm = (rw_mu, rw_w0, rw_w1, rw_w2, rw_a0, rw_a1, rw_a2, rw_kk, rw_ka, rw_rk)
    out_prm = (conv_w, rw_ln_g.reshape(DEPTH, 1, RW_WIDTH), rw_ln_b.reshape(DEPTH, 1, RW_WIDTH), w_out_bf,
               norm_ffn.reshape(DEPTH, 1, D_MODEL), rt_w, rt_b)

    x = jnp.concatenate([x_prompt.reshape(N_CTX_ROWS, D_MODEL), x_sample.reshape(N_LAT_ROWS, D_MODEL)], axis=0)
    keys, vals, states = [], [], []
    for l in range(DEPTH):
        conv, q, k, v, rw, cedge, redge = _inproj(x, mod3, norm_mix, w_in_bf, l)
        cpe, cne = _tile_edges(cedge)
        rpe, rne = _tile_edges(redge)

        sc_c, bonus_c = _rwkv_pre(rw, rpe, rne, rw_prm, l, 0, NT_CTX)
        sc_l, bonus_l = _rwkv_pre(rw, rpe, rne, rw_prm, l, NT_CTX, NT - NT_CTX)
        yf_l, yb_l, yf_c, yb_c, sfin = _rwkv_scan(sc_l.reshape(2, 6, DEC_BATCH, DEC_SEQ, RW_WIDTH),
                                                  sc_c.reshape(2, 6, BATCH, SEQ, RW_WIDTH), s0_lat[l],
                                                  DEC_BATCH, DEC_SEQ, BATCH, SEQ)
        yf = (yf_c.reshape(N_CTX_ROWS, RW_WIDTH), yf_l.reshape(N_LAT_ROWS, RW_WIDTH))
        yb = (yb_c.reshape(N_CTX_ROWS, RW_WIDTH), yb_l.reshape(N_LAT_ROWS, RW_WIDTH))
        bonus = (bonus_c, bonus_l)

        att_c = _ctx_attention(q, k, v, sink_b, l)
        att_l = _lat_attention(q, k, v, cache_k4, cache_v4, cos, sin, sink_b, l)
        att = (att_c, att_l)

        x_mid, h3, route, counts = _outproj(x, conv, cpe, cne, att, yf, yb, bonus, rw, mod3, out_prm, l)
        slots, block_exp, n_used, next_expert = _slot_plan(route, counts)
        xs = _dispatch(slots, h3, xs_zero)
        ys = _experts(block_exp, n_used, next_expert, xs, exp_w1, exp_w3, exp_w2, l)
        x = _combine(slots, x_mid, route, mod3, norm_out, ys, l, l == DEPTH - 1)

        keys.append(k[:N_CTX_ROWS].reshape(BATCH, SEQ, ATT_KV_HEADS, HEAD_DIM))
        vals.append(v[:N_CTX_ROWS].reshape(BATCH, SEQ, ATT_KV_HEADS, HEAD_DIM))
        states.append(_pairs_to_state(sfin))

    y_prompt = x[:N_CTX_ROWS].reshape(BATCH, SEQ, D_MODEL)
    y_sample = x[N_CTX_ROWS:].reshape(DEC_BATCH, DEC_SEQ, D_MODEL)
    return (y_prompt, y_sample, jnp.stack(keys, axis=1), jnp.stack(vals, axis=1), jnp.stack(states, axis=1))
```

```python
import functools

import jax
import jax.numpy as jnp
from jax import lax
from jax.experimental import pallas as pl
from jax.experimental.pallas import tpu as pltpu

F32 = jnp.float32
BF16 = jnp.bfloat16
HIGHEST = lax.Precision.HIGHEST

D_MODEL = 1024
BATCH = 16
SEQ = 256
DEPTH = 4
DEC_BATCH = 4
DEC_SEQ = 2048
PAST_LEN = 256
GRID_W = 64
HEAD_DIM = 64
CONV_CH = 256
ATT_WIDTH = 512
ATT_HEADS = 8
ATT_KV_HEADS = 2
ATT_GQ = 4
KV_WIDTH = 128
WINDOW = 128
ATT_BLOCK = 128
RW_WIDTH = 256
RW_HEADS = 4
LORA = 64
IN_WIDTH = 2816
N_GROUPS = 4
EXP_PER_GROUP = 8
N_EXPERTS = 32
TOP_K = 2
D_EXPERT = 512
ROPE_BASE = 10000.0
NORM_EPS = 1e-6
GN_EPS = 64e-5
DECAY_SCALE = 0.6065306597126334
NEG_INF = -1e30

LANES = 128
SUBLANES = 8
N_CTX_ROWS = BATCH * SEQ
N_LAT_ROWS = DEC_BATCH * DEC_SEQ
N_ROWS = N_CTX_ROWS + N_LAT_ROWS
TM = 256
NT = N_ROWS // TM
NT_CTX = N_CTX_ROWS // TM
TILES_PER_LAT = DEC_SEQ // TM
N_COND = 8
D_CHUNKS = D_MODEL // LANES
MOE_BLOCK = 256
N_ASG = N_ROWS * TOP_K
N_MOE_BLOCKS = N_ASG // MOE_BLOCK + N_EXPERTS
N_SLOTS = N_MOE_BLOCKS * MOE_BLOCK
VMEM_LIMIT = 56 * 1024 * 1024
CONV_W = 3 * CONV_CH
Q_OFF = CONV_W
K_OFF = Q_OFF + ATT_WIDTH
V_OFF = K_OFF + KV_WIDTH
RW_OFF = V_OFF + KV_WIDTH
RW_IN = 4 * RW_WIDTH
MOD_TILE = 1536


def _params(sem, vmem=VMEM_LIMIT):
    return pltpu.CompilerParams(dimension_semantics=sem, vmem_limit_bytes=vmem)


def _cond_of_tile(i):
    return jnp.where(i < NT_CTX, 0, 1 + (i - NT_CTX) // TILES_PER_LAT)


def _mod_spec(layer, j):
    return pl.BlockSpec((1, 1, D_MODEL),
                        lambda i: ((layer * N_COND + _cond_of_tile(i)) * 6 + j, 0, 0))


def _layer_spec(shape, layer):
    nd = len(shape)
    return pl.BlockSpec((None,) + tuple(shape), lambda *_: (layer,) + (0,) * nd)


def _rms(x, g):
    return x * lax.rsqrt(jnp.mean(x * x, axis=-1, keepdims=True) + NORM_EPS) * g


def _bdot(a, b):
    return jnp.dot(a.astype(BF16), b.astype(BF16), preferred_element_type=F32)


def _mod_kernel(c_ref, w_ref, b_ref, o_ref):
    c = c_ref[...]
    s = c * jax.nn.sigmoid(c)
    o_ref[...] = jnp.dot(s, w_ref[...], preferred_element_type=F32, precision=HIGHEST) + b_ref[...]


def _modulation(cond, w_mod, b_mod):
    tn = MOD_TILE
    return pl.pallas_call(
        _mod_kernel,
        grid=(DEPTH, 6 * D_MODEL // tn),
        in_specs=[pl.BlockSpec((N_COND, D_MODEL), lambda l, j: (0, 0)),
                  pl.BlockSpec((None, D_MODEL, tn), lambda l, j: (l, 0, j)),
                  pl.BlockSpec((None, 1, tn), lambda l, j: (l, 0, j))],
        out_specs=pl.BlockSpec((None, N_COND, tn), lambda l, j: (l, 0, j)),
        out_shape=jax.ShapeDtypeStruct((DEPTH, N_COND, 6 * D_MODEL), F32),
        compiler_params=_params(("arbitrary", "arbitrary")),
        name="modulation",
    )(cond, w_mod, b_mod.reshape(DEPTH, 1, 6 * D_MODEL))


def _inproj_kernel(x_ref, sh_ref, sc_ref, g_ref, w_ref, conv_ref, q_ref, k_ref, v_ref, rw_ref, cedge_ref, redge_ref):
    h = _rms(x_ref[...], g_ref[...]) * (1.0 + sc_ref[0]) + sh_ref[0]
    p = jnp.dot(h.astype(BF16), w_ref[...], preferred_element_type=F32)
    conv_ref[...] = p[:, 0:CONV_W]
    q_ref[...] = p[:, Q_OFF:K_OFF]
    k_ref[...] = p[:, K_OFF:V_OFF]
    v_ref[...] = p[:, V_OFF:RW_OFF]
    rw_ref[...] = p[:, RW_OFF:IN_WIDTH]
    cedge_ref[0, 0:1, :] = p[0:1, 0:CONV_W]
    cedge_ref[0, 1:2, :] = p[TM - 1:TM, 0:CONV_W]
    redge_ref[0, 0:1, :] = p[0:1, RW_OFF:RW_OFF + RW_IN]
    redge_ref[0, 1:2, :] = p[TM - 1:TM, RW_OFF:RW_OFF + RW_IN]


def _inproj(x, mod3, norm_mix, w_in_bf, layer):
    widths = (CONV_W, ATT_WIDTH, KV_WIDTH, KV_WIDTH, 5 * RW_WIDTH)
    edges = (CONV_W, RW_IN)
    return pl.pallas_call(
        _inproj_kernel,
        grid=(NT,),
        in_specs=[pl.BlockSpec((TM, D_MODEL), lambda i: (i, 0)),
                  _mod_spec(layer, 0), _mod_spec(layer, 1),
                  _layer_spec((1, D_MODEL), layer),
                  _layer_spec((D_MODEL, IN_WIDTH), layer)],
        out_specs=[pl.BlockSpec((TM, w), lambda i: (i, 0)) for w in widths]
        + [pl.BlockSpec((1, 2, w), lambda i: (i, 0, 0)) for w in edges],
        out_shape=[jax.ShapeDtypeStruct((N_ROWS, w), F32) for w in widths]
        + [jax.ShapeDtypeStruct((NT, 2, w), F32) for w in edges],
        compiler_params=_params(("parallel",)),
        name="inproj",
    )(x, mod3, mod3, norm_mix.reshape(DEPTH, 1, D_MODEL), w_in_bf)


def _tile_edges(edge):
    c = edge.shape[-1]
    first, last = edge[:, 0], edge[:, 1]
    t = jnp.arange(NT)
    lat = t >= NT_CTX
    pos = (t - NT_CTX) % TILES_PER_LAT
    has_prev = (lat & (pos != 0))[:, None]
    has_next = (lat & (pos != TILES_PER_LAT - 1))[:, None]
    prev = jnp.where(has_prev, jnp.roll(last, 1, axis=0), 0.0)
    nxt = jnp.where(has_next, jnp.roll(first, -1, axis=0), 0.0)
    return prev.reshape(NT, 1, c), nxt.reshape(NT, 1, c)


def _shift_rows(x, prev_row, next_row):
    n = x.shape[0]
    row = lax.broadcasted_iota(jnp.int32, x.shape, 0)
    before = jnp.where(row == 0, prev_row, pltpu.roll(x, 1, axis=0))
    after = jnp.where(row == n - 1, next_row, pltpu.roll(x, n - 1, axis=0))
    return before, after


def _split_bf16(x):
    hi = x.astype(BF16)
    return hi, (x - hi.astype(F32)).astype(BF16)


def _head_ones(n):
    r = lax.broadcasted_iota(jnp.int32, (2 * n, n), 0) % n // HEAD_DIM
    c = lax.broadcasted_iota(jnp.int32, (2 * n, n), 1) // HEAD_DIM
    return (r == c).astype(BF16)


def _head_sum(x, ones):
    return jnp.dot(jnp.concatenate(_split_bf16(x), axis=1), ones, preferred_element_type=F32)


def _rwpre_kernel(x_ref, pe_ref, ne_ref, mu_ref, w0_ref, w1_ref, w2_ref, a0_ref, a1_ref, a2_ref,
                  kk_ref, ka_ref, rk_ref, sc_ref, bonus_ref):
    x = x_ref[...]
    before, after = _shift_rows(x, pe_ref[0], ne_ref[0])
    ones = _head_ones(RW_WIDTH)
    bonus = jnp.zeros((TM, RW_WIDTH), F32)
    for d in range(2):
        nbr = before if d == 0 else after
        xd = x + (nbr - x) * mu_ref[d:d + 1, :]
        r, k, v, z = [xd[:, j * RW_WIDTH:(j + 1) * RW_WIDTH] for j in range(4)]
        lw = _bdot(jnp.tanh(_bdot(z, w1_ref[d])), w2_ref[d])
        w = jnp.exp(-DECAY_SCALE * jax.nn.sigmoid(w0_ref[d:d + 1, :] + lw))
        la = _bdot(_bdot(z, a1_ref[d]), a2_ref[d])
        a = jax.nn.sigmoid(a0_ref[d:d + 1, :] + la)
        kk = k * kk_ref[d:d + 1, :]
        kk = kk / jnp.maximum(jnp.sqrt(_head_sum(kk * kk, ones)), 1e-12)
        k2 = k * (1.0 + (a - 1.0) * ka_ref[d:d + 1, :])
        bonus = bonus + _head_sum(r * k2 * rk_ref[d:d + 1, :], ones) * v
        sc_ref[d, 0] = r
        sc_ref[d, 1] = w
        sc_ref[d, 2] = k2
        sc_ref[d, 3] = v
        sc_ref[d, 4] = -kk
        sc_ref[d, 5] = kk * a
    bonus_ref[...] = bonus


def _rwkv_pre(rw, prev_e, next_e, prm, layer, tile0, ntiles):
    (mu, w0, w1, w2, a0, a1, a2, kk, ka, rk) = prm
    nrows = ntiles * TM
    return pl.pallas_call(
        _rwpre_kernel,
        grid=(ntiles,),
        in_specs=[pl.BlockSpec((TM, 4 * RW_WIDTH), lambda i: (i + tile0, 0)),
                  pl.BlockSpec((1, 1, 4 * RW_WIDTH), lambda i: (i + tile0, 0, 0)),
                  pl.BlockSpec((1, 1, 4 * RW_WIDTH), lambda i: (i + tile0, 0, 0)),
                  _layer_spec((2, 4 * RW_WIDTH), layer),
                  _layer_spec((2, RW_WIDTH), layer),
                  _layer_spec((2, RW_WIDTH, LORA), layer),
                  _layer_spec((2, LORA, RW_WIDTH), layer),
                  _layer_spec((2, RW_WIDTH), layer),
                  _layer_spec((2, RW_WIDTH, LORA), layer),
                  _layer_spec((2, LORA, RW_WIDTH), layer),
                  _layer_spec((2, RW_WIDTH), layer),
                  _layer_spec((2, RW_WIDTH), layer),
                  _layer_spec((2, RW_WIDTH), layer)],
        out_specs=[pl.BlockSpec((2, 6, TM, RW_WIDTH), lambda i: (0, 0, i, 0)),
                   pl.BlockSpec((TM, RW_WIDTH), lambda i: (i, 0))],
        out_shape=[jax.ShapeDtypeStruct((2, 6, nrows, RW_WIDTH), F32),
                   jax.ShapeDtypeStruct((nrows, RW_WIDTH), F32)],
        compiler_params=_params(("parallel",)),
        name="rwkv_pre",
    )(rw, prev_e, next_e, mu, w0, w1, w2, a0, a1, a2, kk, ka, rk)


def _half_ones():
    r = lax.broadcasted_iota(jnp.int32, (LANES, LANES), 0) // HEAD_DIM
    c = lax.broadcasted_iota(jnp.int32, (LANES, LANES), 1) // HEAD_DIM
    return (r == c).astype(BF16)


def _scan_kernel(lf_ref, lb_ref, c_ref, s0_ref, ylf_ref, ylb_ref, ycf_ref, ycb_ref, sfin_ref, stl_ref, stc_ref,
                 *, n_lat, n_ctx, tblk):
    tb = pl.program_id(0)

    @pl.when(tb == 0)
    def _():
        stl_ref[...] = s0_ref[...]

    stc_ref[...] = jnp.zeros_like(stc_ref)

    ones = _half_ones()
    vi = lax.broadcasted_iota(jnp.int32, (HEAD_DIM, LANES), 0)
    li = lax.broadcasted_iota(jnp.int32, (HEAD_DIM, LANES), 1)
    diag = vi == (li % HEAD_DIM)
    diag_bf = diag.astype(BF16)
    r_row, w_row, k_row, v_row, a_row, b_row = range(6)
    seqs = [("lat", b, d) for b in range(n_lat) for d in range(2)] + \
           [("ctx", b, d) for b in range(n_ctx) for d in range(2)]
    chains = [(q, p) for q in seqs for p in range(2)]

    def row_of(t, d):
        return t if d == 0 else tblk - 1 - t

    def in_row(q, j, t):
        kind, b, d = q
        if kind == "lat":
            return (lf_ref if d == 0 else lb_ref)[j, b, pl.ds(row_of(t, d), 1), :]
        return c_ref[d, j, b, pl.ds(row_of(t, d), 1), :]

    def y_ref_of(q):
        kind, _, d = q
        if kind == "lat":
            return ylf_ref if d == 0 else ylb_ref
        return ycf_ref if d == 0 else ycb_ref

    def state_ref(q):
        return stl_ref if q[0] == "lat" else stc_ref

    def load_rows(t, which):
        rows = {}
        for q in seqs:
            full = {j: in_row(q, j, t) for j in which}
            for p in range(2):
                rows[(q, p)] = {j: x[:, p * LANES:(p + 1) * LANES] for j, x in full.items()}
        return rows

    def states():
        return {(q, p): state_ref(q)[q[1], q[2], p] for (q, p) in chains}

    def emit_y(t, ss, rows):
        pr = jnp.concatenate([(ss[c] * rows[c][r_row]).astype(BF16) for c in chains], axis=0)
        ybc = jnp.dot(pr, ones, preferred_element_type=F32)
        ys = {}
        for i, c in enumerate(chains):
            blk = ybc[i * HEAD_DIM:(i + 1) * HEAD_DIM]
            ys[c] = jnp.sum(jnp.where(diag, blk, 0.0), axis=0, keepdims=True)
        for q in seqs:
            y_ref_of(q)[q[1], pl.ds(row_of(t, q[2]), 1), :] = jnp.concatenate([ys[(q, 0)], ys[(q, 1)]], axis=1)

    def step(t, carry):
        rows = load_rows(t, (w_row, k_row, v_row, a_row, b_row))
        tp = jnp.maximum(t - 1, 0)
        ss = states()
        emit_y(tp, ss, load_rows(tp, (r_row,)))
        pa = jnp.concatenate([(ss[c] * rows[c][a_row]).astype(BF16) for c in chains], axis=0)
        dv = jnp.concatenate([diag_bf * rows[c][v_row].astype(BF16) for c in chains], axis=0)
        sa = jnp.dot(pa, ones, preferred_element_type=F32)
        vc = jnp.dot(dv, ones, preferred_element_type=F32)
        for i, c in enumerate(chains):
            (q, p), rw = c, rows[c]
            sl = slice(i * HEAD_DIM, (i + 1) * HEAD_DIM)
            state_ref(q)[q[1], q[2], p] = ss[c] * rw[w_row] + sa[sl] * rw[b_row] + vc[sl] * rw[k_row]
        return carry

    lax.fori_loop(0, tblk, step, 0)
    emit_y(tblk - 1, states(), load_rows(tblk - 1, (r_row,)))
    sfin_ref[...] = stc_ref[...]


def _rwkv_scan(sc_lat, sc_ctx, s0_lat, n_lat, lat_len, n_ctx_total, tblk):
    ntb = lat_len // tblk
    n_ctx = n_ctx_total // ntb
    kern = functools.partial(_scan_kernel, n_lat=n_lat, n_ctx=n_ctx, tblk=tblk)
    st_shape = (2, 2, HEAD_DIM, LANES)
    return pl.pallas_call(
        kern,
        grid=(ntb,),
        in_specs=[pl.BlockSpec((None, 6, n_lat, tblk, RW_WIDTH), lambda t: (0, 0, 0, t, 0)),
                  pl.BlockSpec((None, 6, n_lat, tblk, RW_WIDTH), lambda t: (1, 0, 0, ntb - 1 - t, 0)),
                  pl.BlockSpec((2, 6, n_ctx, tblk, RW_WIDTH), lambda t: (0, 0, t, 0, 0)),
                  pl.BlockSpec((n_lat,) + st_shape, lambda t: (0, 0, 0, 0, 0))],
        out_specs=[pl.BlockSpec((n_lat, tblk, RW_WIDTH), lambda t: (0, t, 0)),
                   pl.BlockSpec((n_lat, tblk, RW_WIDTH), lambda t: (0, ntb - 1 - t, 0)),
                   pl.BlockSpec((n_ctx, tblk, RW_WIDTH), lambda t: (t, 0, 0)),
                   pl.BlockSpec((n_ctx, tblk, RW_WIDTH), lambda t: (t, 0, 0)),
                   pl.BlockSpec((n_ctx,) + st_shape, lambda t: (t, 0, 0, 0, 0))],
        out_shape=[jax.ShapeDtypeStruct((n_lat, lat_len, RW_WIDTH), F32),
                   jax.ShapeDtypeStruct((n_lat, lat_len, RW_WIDTH), F32),
                   jax.ShapeDtypeStruct((n_ctx_total, tblk, RW_WIDTH), F32),
                   jax.ShapeDtypeStruct((n_ctx_total, tblk, RW_WIDTH), F32),
                   jax.ShapeDtypeStruct((n_ctx_total,) + st_shape, F32)],
        scratch_shapes=[pltpu.VMEM((n_lat,) + st_shape, F32), pltpu.VMEM((n_ctx,) + st_shape, F32)],
        compiler_params=_params(("arbitrary",)),
        name="rwkv_scan",
    )(sc_lat, sc_lat, sc_ctx, s0_lat)


def _state_to_pairs(s):
    b = s.shape[0]
    return s.reshape(b, 2, 2, 2, HEAD_DIM, HEAD_DIM).transpose(0, 1, 2, 4, 3, 5).reshape(b, 2, 2, HEAD_DIM, LANES)


def _pairs_to_state(s):
    b = s.shape[0]
    return s.reshape(b, 2, 2, HEAD_DIM, 2, HEAD_DIM).transpose(0, 1, 2, 4, 3, 5).reshape(b, 2, RW_HEADS, HEAD_DIM, HEAD_DIM)


def _rope(x, cos, sin):
    n = x.shape[1]
    m = n // LANES
    if m > 1:
        cos = jnp.concatenate([cos] * m, axis=1)
        sin = jnp.concatenate([sin] * m, axis=1)
    lane = lax.broadcasted_iota(jnp.int32, x.shape, 1)
    swapped = jnp.where((lane % 32) < 16, pltpu.roll(x, n - 16, axis=1), pltpu.roll(x, 16, axis=1))
    return x * cos + swapped * sin


def _attend(q, keys, vals, biases, sink_ref, o_ref):
    nq = q.shape[0]
    qb = (q * (HEAD_DIM ** -0.5)).astype(BF16)
    kbs = [k.astype(BF16) for k in keys]
    vbs = [v.astype(BF16) for v in vals]
    tiled = [None if b is None else jnp.concatenate([b] * ATT_GQ, axis=0) for b in biases]
    for hk in range(ATT_KV_HEADS):
        heads = range(hk * ATT_GQ, (hk + 1) * ATT_GQ)
        kv = slice(hk * HEAD_DIM, (hk + 1) * HEAD_DIM)
        qg = jnp.concatenate([qb[:, h * HEAD_DIM:(h + 1) * HEAD_DIM] for h in heads], axis=0)
        sink = jnp.concatenate([jnp.broadcast_to(sink_ref[h:h + 1, 0:1], (nq, 1)) for h in heads], axis=0)
        ss = []
        for kb, bias in zip(kbs, tiled):
            s = lax.dot_general(qg, kb[:, kv], (((1,), (1,)), ((), ())), preferred_element_type=F32)
            ss.append(s if bias is None else s + bias)
        m_lanes = ss[0]
        for s in ss[1:]:
            m_lanes = jnp.maximum(m_lanes, s)
        m = jnp.maximum(jnp.max(m_lanes, axis=-1, keepdims=True), sink)
        l_lanes = None
        acc = jnp.zeros((ATT_GQ * nq, HEAD_DIM), F32)
        for s, vb in zip(ss, vbs):
            p = jnp.exp(s - m)
            l_lanes = p if l_lanes is None else l_lanes + p
            acc = acc + jnp.dot(p.astype(BF16), vb[:, kv], preferred_element_type=F32)
        out = acc / (jnp.sum(l_lanes, axis=-1, keepdims=True) + jnp.exp(sink - m))
        for g, h in enumerate(heads):
            o_ref[:, h * HEAD_DIM:(h + 1) * HEAD_DIM] = out[g * nq:(g + 1) * nq]


def _ctx_attn_kernel(q_ref, k_ref, v_ref, sink_ref, o_ref):
    chunks = range(0, SEQ, ATT_BLOCK)
    _attend(q_ref[...], [k_ref[c:c + ATT_BLOCK, :] for c in chunks], [v_ref[c:c + ATT_BLOCK, :] for c in chunks],
            [None for _ in chunks], sink_ref, o_ref)


def _ctx_attention(q, k, v, sink_b, layer):
    nq = SEQ // ATT_BLOCK
    return pl.pallas_call(
        _ctx_attn_kernel,
        grid=(BATCH, nq),
        in_specs=[pl.BlockSpec((ATT_BLOCK, ATT_WIDTH), lambda b, i: (b * nq + i, 0)),
                  pl.BlockSpec((SEQ, KV_WIDTH), lambda b, i: (b, 0)),
                  pl.BlockSpec((SEQ, KV_WIDTH), lambda b, i: (b, 0)),
                  pl.BlockSpec((None, ATT_HEADS, LANES), lambda b, i: (layer, 0, 0))],
        out_specs=pl.BlockSpec((ATT_BLOCK, ATT_WIDTH), lambda b, i: (b * nq + i, 0)),
        out_shape=jax.ShapeDtypeStruct((N_CTX_ROWS, ATT_WIDTH), F32),
        compiler_params=_params(("parallel", "parallel")),
        name="ctx_attention",
    )(q, k, v, sink_b)


def _lat_attn_kernel(q_ref, k_ref, v_ref, ck_ref, cv_ref, cos_ref, sin_ref, sink_ref, o_ref):
    n = pl.program_id(1)
    nb = pl.num_programs(1)
    r0 = pl.multiple_of(n * ATT_BLOCK, ATT_BLOCK)
    q = _rope(q_ref[...], cos_ref[pl.ds(r0, ATT_BLOCK), :], sin_ref[pl.ds(r0, ATT_BLOCK), :])
    qi = lax.broadcasted_iota(jnp.int32, (ATT_BLOCK, ATT_BLOCK), 0)
    ki = lax.broadcasted_iota(jnp.int32, (ATT_BLOCK, ATT_BLOCK), 1)
    keys, vals, masks = [], [], []
    for j in (-1, 0, 1):
        blk = jnp.clip(n + j, 0, nb - 1)
        k0 = pl.multiple_of(blk * ATT_BLOCK, ATT_BLOCK)
        kb = _rope(k_ref[pl.ds(k0, ATT_BLOCK), :], cos_ref[pl.ds(k0, ATT_BLOCK), :],
                   sin_ref[pl.ds(k0, ATT_BLOCK), :])
        keys.append(kb)
        vals.append(v_ref[pl.ds(k0, ATT_BLOCK), :])
        rel = ki + j * ATT_BLOCK - qi
        reach = jnp.where((n + j >= 0) & (n + j < nb), WINDOW, -1)
        masks.append(jnp.where(jnp.abs(rel) <= reach, 0.0, NEG_INF))
    for c in range(0, PAST_LEN, ATT_BLOCK):
        keys.append(ck_ref[c:c + ATT_BLOCK, :])
        vals.append(cv_ref[c:c + ATT_BLOCK, :])
        masks.append(None)
    _attend(q, keys, vals, masks, sink_ref, o_ref)


def _lat_attention(q, k, v, cache_k4, cache_v4, cos, sin, sink_b, layer):
    nb = DEC_SEQ // ATT_BLOCK
    qoff = N_CTX_ROWS // ATT_BLOCK
    soff = N_CTX_ROWS // DEC_SEQ
    return pl.pallas_call(
        _lat_attn_kernel,
        grid=(DEC_BATCH, nb),
        in_specs=[pl.BlockSpec((ATT_BLOCK, ATT_WIDTH), lambda b, i: (qoff + b * nb + i, 0)),
                  pl.BlockSpec((DEC_SEQ, KV_WIDTH), lambda b, i: (soff + b, 0)),
                  pl.BlockSpec((DEC_SEQ, KV_WIDTH), lambda b, i: (soff + b, 0)),
                  pl.BlockSpec((None, None, PAST_LEN, KV_WIDTH), lambda b, i: (b, layer, 0, 0)),
                  pl.BlockSpec((None, None, PAST_LEN, KV_WIDTH), lambda b, i: (b, layer, 0, 0)),
                  pl.BlockSpec((DEC_SEQ, LANES), lambda b, i: (0, 0)),
                  pl.BlockSpec((DEC_SEQ, LANES), lambda b, i: (0, 0)),
                  pl.BlockSpec((None, ATT_HEADS, LANES), lambda b, i: (layer, 0, 0))],
        out_specs=pl.BlockSpec((ATT_BLOCK, ATT_WIDTH), lambda b, i: (b * nb + i, 0)),
        out_shape=jax.ShapeDtypeStruct((N_LAT_ROWS, ATT_WIDTH), F32),
        compiler_params=_params(("parallel", "parallel")),
        name="lat_attention",
    )(q, k, v, cache_k4, cache_v4, cos, sin, sink_b)


def _rope_tables():
    pos = jnp.arange(DEC_SEQ)
    rows = (pos // GRID_W).astype(F32)
    cols = (pos % GRID_W).astype(F32)
    inv = ROPE_BASE ** (-jnp.arange(0, 32, 2, dtype=F32) / 32)
    lane = jnp.arange(LANES)
    posl = jnp.where(((lane % HEAD_DIM) // 32 == 0)[None, :], rows[:, None], cols[:, None])
    ang = posl * inv[lane % 16][None, :]
    sign = jnp.where((lane % 32) < 16, -1.0, 1.0)[None, :]
    return jnp.cos(ang), jnp.sin(ang) * sign


def _outproj_kernel(x_ref, conv_ref, cpe_ref, cne_ref, attc_ref, attl_ref, yfc_ref, yfl_ref, ybc_ref, ybl_ref,
                    bonc_ref, bonl_ref, g_ref,
                    cw_ref, lng_ref, lnb_ref, wout_ref, gate_ref, sh_ref, sc_ref, nf_ref, rtw_ref, rtb_ref,
                    xo_ref, h3_ref, route_ref, counts_ref, cnt_ref):
    is_ctx = pl.program_id(0) < NT_CTX
    att = jnp.where(is_ctx, attc_ref[...], attl_ref[...])
    ysum = jnp.where(is_ctx, yfc_ref[...] + ybc_ref[...], yfl_ref[...] + ybl_ref[...])
    bonus = jnp.where(is_ctx, bonc_ref[...], bonl_ref[...])
    conv = conv_ref[...]
    c_gate, h_in = slice(CONV_CH, 2 * CONV_CH), slice(2 * CONV_CH, CONV_W)
    u = conv[:, c_gate] * conv[:, h_in]
    pe = cpe_ref[0]
    ne = cne_ref[0]
    before, after = _shift_rows(u, pe[:, c_gate] * pe[:, h_in], ne[:, c_gate] * ne[:, h_in])
    y_conv = conv[:, 0:CONV_CH] * (before * cw_ref[0:1, :] + u * cw_ref[1:2, :] + after * cw_ref[2:3, :])

    ones = _head_ones(RW_WIDTH)
    inv_hd = 1.0 / HEAD_DIM
    mean = _head_sum(ysum, ones) * inv_hd
    cen = ysum - mean
    var = _head_sum(cen * cen, ones) * inv_hd
    yn = cen * lax.rsqrt(var + GN_EPS) * lng_ref[...] + lnb_ref[...]
    y_rw = (yn + bonus) * jax.nn.sigmoid(g_ref[...])

    att_end = CONV_CH + ATT_WIDTH
    mix = (jnp.dot(y_conv.astype(BF16), wout_ref[0:CONV_CH, :], preferred_element_type=F32)
           + jnp.dot(att.astype(BF16), wout_ref[CONV_CH:att_end, :], preferred_element_type=F32)
           + jnp.dot(y_rw.astype(BF16), wout_ref[att_end:D_MODEL, :], preferred_element_type=F32))
    x = x_ref[...] + gate_ref[0] * mix
    xo_ref[...] = x
    h2 = _rms(x, nf_ref[...]) * (1.0 + sc_ref[0]) + sh_ref[0]
    for j in range(D_CHUNKS):
        h3_ref[pl.ds(j, TM, stride=D_CHUNKS), :] = h2[:, j * LANES:(j + 1) * LANES]

    h_hi, h_lo = _split_bf16(h2)
    logits = jnp.dot(jnp.concatenate([h_hi, h_lo, h_hi], axis=1), rtw_ref[...],
                     preferred_element_type=F32) + rtb_ref[...]
    lane = lax.broadcasted_iota(jnp.int32, logits.shape, 1).astype(F32)
    big = float(1 << 20)
    is_g = lane < N_GROUPS
    gmax = jnp.max(jnp.where(is_g, logits, NEG_INF), axis=-1, keepdims=True)
    gsum = jnp.sum(jnp.where(is_g, jnp.exp(logits - gmax), 0.0), axis=-1, keepdims=True)
    g_top = 1.0 / gsum
    g_idx = jnp.min(jnp.where(is_g & (logits == gmax), lane, big), axis=-1, keepdims=True)
    lo = N_GROUPS + g_idx * EXP_PER_GROUP
    in_grp = (lane >= lo) & (lane < lo + EXP_PER_GROUP)
    emax = jnp.max(jnp.where(in_grp, logits, NEG_INF), axis=-1, keepdims=True)
    pe_ = jnp.where(in_grp, jnp.exp(logits - emax), -1.0)
    p1 = jnp.max(pe_, axis=-1, keepdims=True)
    l1 = jnp.min(jnp.where(pe_ == p1, lane, big), axis=-1, keepdims=True)
    pe2 = jnp.where(lane == l1, -1.0, pe_)
    p2 = jnp.max(pe2, axis=-1, keepdims=True)
    l2 = jnp.min(jnp.where(pe2 == p2, lane, big), axis=-1, keepdims=True)
    wsum = p1 + p2
    w1 = g_top * p1 / wsum
    w2 = g_top * p2 / wsum
    e1 = l1 - N_GROUPS
    e2 = l2 - N_GROUPS

    @pl.when(pl.program_id(0) == 0)
    def _():
        cnt_ref[...] = jnp.zeros_like(cnt_ref)

    hit1 = lane == l1 - N_GROUPS
    hit2 = lane == l2 - N_GROUPS
    onehot = jnp.where(hit1 | hit2, 1.0, 0.0)
    ri = lax.broadcasted_iota(jnp.int32, (TM, TM), 0)
    ci = lax.broadcasted_iota(jnp.int32, (TM, TM), 1)
    earlier = jnp.where(ci < ri, 1.0, 0.0).astype(BF16)
    before = jnp.dot(earlier, onehot.astype(BF16), preferred_element_type=F32) + cnt_ref[...]
    r1 = jnp.sum(jnp.where(hit1, before, 0.0), axis=-1, keepdims=True)
    r2 = jnp.sum(jnp.where(hit2, before, 0.0), axis=-1, keepdims=True)
    cnt_ref[...] = cnt_ref[...] + jnp.sum(onehot, axis=0, keepdims=True)
    counts_ref[...] = jnp.broadcast_to(cnt_ref[...], counts_ref.shape)

    route = jnp.zeros_like(logits)
    for j, col in enumerate((e1, e2, w1, w2, r1, r2)):
        route = jnp.where(lane == j, col, route)
    route_ref[...] = route


def _outproj(x, conv, cpe, cne, att, yf, yb, bonus, rw, mod3, prm, layer):
    conv_w, ln_g, ln_b, w_out_bf, norm_ffn, rt_w, rt_b = prm
    row = lambda w: pl.BlockSpec((TM, w), lambda i: (i, 0))
    ctx = lambda w: pl.BlockSpec((TM, w), lambda i: (jnp.minimum(i, NT_CTX - 1), 0))
    lat = lambda w: pl.BlockSpec((TM, w), lambda i: (jnp.maximum(i - NT_CTX, 0), 0))
    edge = pl.BlockSpec((1, 1, CONV_W), lambda i: (i, 0, 0))
    return pl.pallas_call(
        _outproj_kernel,
        grid=(NT,),
        in_specs=[row(D_MODEL), row(CONV_W), edge, edge, ctx(ATT_WIDTH), lat(ATT_WIDTH), ctx(RW_WIDTH), lat(RW_WIDTH),
                  ctx(RW_WIDTH), lat(RW_WIDTH), ctx(RW_WIDTH), lat(RW_WIDTH),
                  pl.BlockSpec((TM, RW_WIDTH), lambda i: (i, 4)),
                  _layer_spec((3, CONV_CH), layer), _layer_spec((1, RW_WIDTH), layer),
                  _layer_spec((1, RW_WIDTH), layer), _layer_spec((D_MODEL, D_MODEL), layer),
                  _mod_spec(layer, 2), _mod_spec(layer, 3), _mod_spec(layer, 4),
                  _layer_spec((1, D_MODEL), layer), _layer_spec((3 * D_MODEL, LANES), layer),
                  _layer_spec((1, LANES), layer)],
        out_specs=[row(D_MODEL), pl.BlockSpec((TM * D_CHUNKS, LANES), lambda i: (i, 0)), row(LANES),
                   pl.BlockSpec((SUBLANES, LANES), lambda i: (0, 0))],
        out_shape=[jax.ShapeDtypeStruct((N_ROWS, D_MODEL), F32),
                   jax.ShapeDtypeStruct((N_ROWS * D_CHUNKS, LANES), F32),
                   jax.ShapeDtypeStruct((N_ROWS, LANES), F32),
                   jax.ShapeDtypeStruct((SUBLANES, LANES), F32)],
        scratch_shapes=[pltpu.VMEM((1, LANES), F32)],
        compiler_params=_params(("arbitrary",)),
        name="outproj_router",
    )(x, conv, cpe, cne, *att, *yf, *yb, *bonus, rw, conv_w, ln_g, ln_b, w_out_bf, mod3, mod3, mod3,
      norm_ffn, rt_w, rt_b)


def _slot_kernel(route_ref, start_ref, o_ref):
    route = route_ref[...]
    lane = lax.broadcasted_iota(jnp.int32, route.shape, 1).astype(F32)
    start = start_ref[...]
    cols = [route[:, 4 + k:5 + k] + jnp.sum(jnp.where(lane == route[:, k:k + 1], start, 0.0), axis=-1, keepdims=True)
            for k in range(TOP_K)]
    o_ref[...] = jnp.where(lane == 0.0, cols[0], jnp.where(lane == 1.0, cols[1], 0.0)).astype(jnp.int32)


def _slot_plan(route, counts):
    counts = counts[0, 0:N_EXPERTS].astype(jnp.int32)
    padded = (counts + MOE_BLOCK - 1) // MOE_BLOCK * MOE_BLOCK
    pad_end = jnp.cumsum(padded)
    pad_start = pad_end - padded
    start_row = jnp.zeros((1, LANES), F32).at[0, 0:N_EXPERTS].set(pad_start.astype(F32))
    slots = pl.pallas_call(
        _slot_kernel,
        grid=(N_ROWS // DEC_SEQ,),
        in_specs=[pl.BlockSpec((DEC_SEQ, LANES), lambda i: (i, 0)), pl.BlockSpec((1, LANES), lambda i: (0, 0))],
        out_specs=pl.BlockSpec((DEC_SEQ, LANES), lambda i: (i, 0)),
        out_shape=jax.ShapeDtypeStruct((N_ROWS, LANES), jnp.int32),
        compiler_params=_params(("parallel",)),
        name="moe_slots",
    )(route, start_row)[:, 0:TOP_K]
    blk0 = jnp.arange(N_MOE_BLOCKS, dtype=jnp.int32) * MOE_BLOCK
    block_exp = jnp.sum((pad_end[None, :] <= blk0[:, None]).astype(jnp.int32), axis=1)
    block_exp = jnp.minimum(block_exp, N_EXPERTS - 1)
    n_used = (pad_end[-1] // MOE_BLOCK).astype(jnp.int32)
    present = jnp.where(counts > 0, jnp.arange(N_EXPERTS, dtype=jnp.int32), N_EXPERTS)
    later = jnp.flip(lax.cummin(jnp.flip(present)))
    next_expert = jnp.concatenate([later[1:], jnp.full((1,), N_EXPERTS, jnp.int32)])
    return slots.reshape(-1), block_exp, n_used.reshape(1), next_expert


def _dispatch_kernel(slots_ref, h3_ref, xs_in_ref, xs_ref, sem):
    del xs_in_ref
    i = pl.program_id(0)
    base = i * (TM * TOP_K)

    def row_copy(r, s):
        src = h3_ref.at[pl.ds(pl.multiple_of(r * D_CHUNKS, D_CHUNKS), D_CHUNKS)]
        dst = xs_ref.at[pl.ds(pl.multiple_of(s * D_CHUNKS, D_CHUNKS), D_CHUNKS)]
        return pltpu.make_async_copy(src, dst, sem)

    def issue(r, c):
        for k in range(TOP_K):
            row_copy(r, slots_ref[base + r * TOP_K + k]).start(priority=k)
        return c

    lax.fori_loop(0, TM, issue, 0)

    def drain(r, c):
        for k in range(TOP_K):
            row_copy(0, 0).wait()
        return c

    lax.fori_loop(0, TM, drain, 0)


def _dispatch(slots, h3, xs_prev):
    return pl.pallas_call(
        _dispatch_kernel,
        grid_spec=pltpu.PrefetchScalarGridSpec(
            num_scalar_prefetch=1,
            grid=(NT,),
            in_specs=[pl.BlockSpec((TM * D_CHUNKS, LANES), lambda i, s: (i, 0)),
                      pl.BlockSpec(memory_space=pl.ANY)],
            out_specs=pl.BlockSpec(memory_space=pl.ANY),
            scratch_shapes=[pltpu.SemaphoreType.DMA(())]),
        out_shape=jax.ShapeDtypeStruct((N_SLOTS * D_CHUNKS, LANES), F32),
        input_output_aliases={2: 0},
        compiler_params=_params(("arbitrary",)),
        name="moe_dispatch",
    )(slots, h3, xs_prev)


def _expert_kernel(be_ref, nu_ref, nxt_ref, xs_ref, w1_hbm, w3_hbm, w2_hbm, ys_ref,
                   w1f, w3f, w2f, w1b, w3b, w2b, sems, turn_ref, *, layer):
    i = pl.program_id(0)
    n_used = nu_ref[0]
    blk = jnp.minimum(i, n_used - 1)
    expert = be_ref[blk]
    prev_expert = be_ref[jnp.maximum(blk - 1, 0)]

    def fetch(e, slot):
        return [pltpu.make_async_copy(hbm.at[layer, e], buf.at[slot], sems.at[slot])
                for hbm, buf in ((w1_hbm, w1f), (w3_hbm, w3f), (w2_hbm, w2f))]

    @pl.when(i == 0)
    def _():
        turn_ref[0] = 0
        for cp in fetch(expert, 0):
            cp.start()

    @pl.when((i < n_used) & ((i == 0) | (expert != prev_expert)))
    def _():
        slot = turn_ref[0] % 2
        for cp in fetch(expert, slot):
            cp.wait()
        nxt = nxt_ref[expert]

        @pl.when(nxt < N_EXPERTS)
        def _():
            for cp in fetch(nxt, 1 - slot):
                cp.start()

        w1b[...] = w1f[slot].astype(BF16)
        w3b[...] = w3f[slot].astype(BF16)
        w2b[...] = w2f[slot].astype(BF16)
        turn_ref[0] = turn_ref[0] + 1

    @pl.when(i < n_used)
    def _():
        x = jnp.concatenate([xs_ref[pl.ds(j, MOE_BLOCK, stride=D_CHUNKS), :].astype(BF16)
                             for j in range(D_CHUNKS)], axis=1)
        a = jnp.dot(x, w1b[...], preferred_element_type=F32)
        b = jnp.dot(x, w3b[...], preferred_element_type=F32)
        hid = (a * jax.nn.sigmoid(a) * b).astype(BF16)
        y = jnp.dot(hid, w2b[...], preferred_element_type=F32)
        for j in range(D_CHUNKS):
            ys_ref[pl.ds(j, MOE_BLOCK, stride=D_CHUNKS), :] = y[:, j * LANES:(j + 1) * LANES]

    @pl.when(i >= n_used)
    def _():
        ys_ref[...] = jnp.zeros_like(ys_ref)


def _experts(block_exp, n_used, next_expert, xs, w1, w3, w2, layer):
    rows = pl.BlockSpec((MOE_BLOCK * D_CHUNKS, LANES), lambda i, be, nu, nx: (i, 0))
    hbm = pl.BlockSpec(memory_space=pl.ANY)
    return pl.pallas_call(
        functools.partial(_expert_kernel, layer=layer),
        grid_spec=pltpu.PrefetchScalarGridSpec(
            num_scalar_prefetch=3,
            grid=(N_MOE_BLOCKS,),
            in_specs=[rows, hbm, hbm, hbm],
            out_specs=rows,
            scratch_shapes=[pltpu.VMEM((2, D_MODEL, D_EXPERT), F32), pltpu.VMEM((2, D_MODEL, D_EXPERT), F32),
                            pltpu.VMEM((2, D_EXPERT, D_MODEL), F32),
                            pltpu.VMEM((D_MODEL, D_EXPERT), BF16), pltpu.VMEM((D_MODEL, D_EXPERT), BF16),
                            pltpu.VMEM((D_EXPERT, D_MODEL), BF16),
                            pltpu.SemaphoreType.DMA((2,)), pltpu.SMEM((1,), jnp.int32)]),
        out_shape=jax.ShapeDtypeStruct((N_SLOTS * D_CHUNKS, LANES), F32),
        compiler_params=_params(("arbitrary",)),
        name="moe_experts",
    )(block_exp, n_used, next_expert, xs, w1, w3, w2)


def _combine_kernel(slots_ref, x_ref, route_ref, gate_ref, g_ref, ys_ref, o_ref, buf, sem, *, final_norm):
    i = pl.program_id(0)
    base = i * (TM * TOP_K)

    def row_copy(s, k, r):
        src = ys_ref.at[pl.ds(pl.multiple_of(s * D_CHUNKS, D_CHUNKS), D_CHUNKS)]
        dst = buf.at[k, pl.ds(pl.multiple_of(r * D_CHUNKS, D_CHUNKS), D_CHUNKS)]
        return pltpu.make_async_copy(src, dst, sem)

    def issue(r, c):
        for k in range(TOP_K):
            row_copy(slots_ref[base + r * TOP_K + k], k, r).start(priority=k)
        return c

    lax.fori_loop(0, TM, issue, 0)

    def drain(r, c):
        for k in range(TOP_K):
            row_copy(0, k, 0).wait()
        return c

    lax.fori_loop(0, TM, drain, 0)
    route = route_ref[...]
    w1 = route[:, 2:3]
    w2 = route[:, 3:4]
    y = jnp.concatenate([w1 * buf[0, pl.ds(j, TM, stride=D_CHUNKS), :] + w2 * buf[1, pl.ds(j, TM, stride=D_CHUNKS), :]
                         for j in range(D_CHUNKS)], axis=1)
    x = x_ref[...] + gate_ref[0] * y
    if final_norm:
        x = _rms(x, g_ref[...])
    o_ref[...] = x


def _combine(slots, x, route, mod3, norm_out, ys, layer, final_norm):
    kern = functools.partial(_combine_kernel, final_norm=final_norm)
    return pl.pallas_call(
        kern,
        grid_spec=pltpu.PrefetchScalarGridSpec(
            num_scalar_prefetch=1,
            grid=(NT,),
            in_specs=[pl.BlockSpec((TM, D_MODEL), lambda i, s: (i, 0)),
                      pl.BlockSpec((TM, LANES), lambda i, s: (i, 0)),
                      pl.BlockSpec((1, 1, D_MODEL),
                                   lambda i, s: ((layer * N_COND + _cond_of_tile(i)) * 6 + 5, 0, 0)),
                      pl.BlockSpec((1, D_MODEL), lambda i, s: (0, 0)),
                      pl.BlockSpec(memory_space=pl.ANY)],
            out_specs=pl.BlockSpec((TM, D_MODEL), lambda i, s: (i, 0)),
            scratch_shapes=[pltpu.VMEM((TOP_K, TM * D_CHUNKS, LANES), F32),
                            pltpu.SemaphoreType.DMA(())]),
        out_shape=jax.ShapeDtypeStruct((N_ROWS, D_MODEL), F32),
        compiler_params=_params(("arbitrary",)),
        name="moe_combine",
    )(slots, x, route, mod3, norm_out.reshape(1, D_MODEL), ys)


def kernel(x_prompt, x_sample, cache_k, cache_v, state_wkv, c, c_ctx, w_mod, b_mod, norm_mix, norm_ffn,
           norm_out, w_in, w_out, conv_w, attn_sink, rw_mu, rw_w0, rw_w1, rw_w2, rw_a0, rw_a1, rw_a2,
           rw_kk, rw_ka, rw_rk, rw_ln_g, rw_ln_b, rt_group_w, rt_group_b, rt_exp_w, rt_exp_b,
           exp_w1, exp_w3, exp_w2):
    cond = jnp.concatenate([c_ctx[None, :], c, jnp.zeros((N_COND - 1 - DEC_BATCH, D_MODEL), F32)], axis=0)
    mod3 = _modulation(cond, w_mod, b_mod).reshape(DEPTH * N_COND * 6, 1, D_MODEL)

    w_in_bf = w_in.astype(BF16)
    w_out_bf = w_out.astype(BF16)
    pad = LANES - N_GROUPS - N_EXPERTS
    rt_w = jnp.concatenate([rt_group_w, rt_exp_w, jnp.zeros((DEPTH, D_MODEL, pad), F32)], axis=-1)
    rt_hi, rt_lo = _split_bf16(rt_w)
    rt_w = jnp.concatenate([rt_hi, rt_hi, rt_lo], axis=1)
    rt_b = jnp.concatenate([rt_group_b, rt_exp_b, jnp.zeros((DEPTH, pad), F32)], axis=-1).reshape(DEPTH, 1, LANES)
    sink_b = jnp.broadcast_to(attn_sink[:, :, None], (DEPTH, ATT_HEADS, LANES))
    cos, sin = _rope_tables()
    cache_k4 = cache_k.reshape(DEC_BATCH, DEPTH, PAST_LEN, KV_WIDTH)
    cache_v4 = cache_v.reshape(DEC_BATCH, DEPTH, PAST_LEN, KV_WIDTH)
    s0_lat = _state_to_pairs(state_wkv.transpose(1, 0, 2, 3, 4, 5).reshape(DEPTH * DEC_BATCH, 2, RW_HEADS, HEAD_DIM, HEAD_DIM))
    s0_lat = s0_lat.reshape(DEPTH, DEC_BATCH, 2, 2, HEAD_DIM, LANES)
    xs = jnp.zeros((N_SLOTS * D_CHUNKS, LANES), F32)
    rw_prm = (rw_mu, rw_w0, rw_w1, rw_w2, rw_a0, rw_a1, rw_a2, rw_kk, rw_ka, rw_rk)
    out_prm = (conv_w, rw_ln_g.reshape(DEPTH, 1, RW_WIDTH), rw_ln_b.reshape(DEPTH, 1, RW_WIDTH), w_out_bf,
               norm_ffn.reshape(DEPTH, 1, D_MODEL), rt_w, rt_b)

    x = jnp.concatenate([x_prompt.reshape(N_CTX_ROWS, D_MODEL), x_sample.reshape(N_LAT_ROWS, D_MODEL)], axis=0)
    keys, vals, states = [], [], []
    for l in range(DEPTH):
        conv, q, k, v, rw, cedge, redge = _inproj(x, mod3, norm_mix, w_in_bf, l)
        cpe, cne = _tile_edges(cedge)
        rpe, rne = _tile_edges(redge)

        sc_c, bonus_c = _rwkv_pre(rw, rpe, rne, rw_prm, l, 0, NT_CTX)
        sc_l, bonus_l = _rwkv_pre(rw, rpe, rne, rw_prm, l, NT_CTX, NT - NT_CTX)
        yf_l, yb_l, yf_c, yb_c, sfin = _rwkv_scan(sc_l.reshape(2, 6, DEC_BATCH, DEC_SEQ, RW_WIDTH),
                                                  sc_c.reshape(2, 6, BATCH, SEQ, RW_WIDTH), s0_lat[l],
                                                  DEC_BATCH, DEC_SEQ, BATCH, SEQ)
        yf = (yf_c.reshape(N_CTX_ROWS, RW_WIDTH), yf_l.reshape(N_LAT_ROWS, RW_WIDTH))
        yb = (yb_c.reshape(N_CTX_ROWS, RW_WIDTH), yb_l.reshape(N_LAT_ROWS, RW_WIDTH))
        bonus = (bonus_c, bonus_l)

        att_c = _ctx_attention(q, k, v, sink_b, l)
        att_l = _lat_attention(q, k, v, cache_k4, cache_v4, cos, sin, sink_b, l)
        att = (att_c, att_l)

        x_mid, h3, route, counts = _outproj(x, conv, cpe, cne, att, yf, yb, bonus, rw, mod3, out_prm, l)
        slots, block_exp, n_used, next_expert = _slot_plan(route, counts)
        xs = _dispatch(slots, h3, xs)
        ys = _experts(block_exp, n_used, next_expert, xs, exp_w1, exp_w3, exp_w2, l)
        x = _combine(slots, x_mid, route, mod3, norm_out, ys, l, l == DEPTH - 1)

        keys.append(k[:N_CTX_ROWS].reshape(BATCH, SEQ, ATT_KV_HEADS, HEAD_DIM))
        vals.append(v[:N_CTX_ROWS].reshape(BATCH, SEQ, ATT_KV_HEADS, HEAD_DIM))
        states.append(_pairs_to_state(sfin))

    y_prompt = x[:N_CTX_ROWS].reshape(BATCH, SEQ, D_MODEL)
    y_sample = x[N_CTX_ROWS:].reshape(DEC_BATCH, DEC_SEQ, D_MODEL)
    return (y_prompt, y_sample, jnp.stack(keys, axis=1), jnp.stack(vals, axis=1), jnp.stack(states, axis=1))
```
